```python
import jax, jax.numpy as jnp
from jax import lax
import numpy as np

D_MODEL = 1024
BATCH = 32
SEQ = 2048
DEPTH = 2

N_MEM = 256
EPS = 1e-6
A_WIDTH = D_MODEL // 2
A_CONV = 3
B_HEADS = 8
B_HEAD_DIM = D_MODEL // (2 * B_HEADS)
DILATED_PATTERN = ((128, 1), (512, 4), (2048, 16))
C_HEADS = 8
C_HEAD_DIM = D_MODEL // C_HEADS
C_CONV = 4
C_CHUNK = 64
X_HEADS = 4
X_HEAD_DIM = 64
D_FF = 2816
N_EXPERTS = 8
TOP_K = 2
D_FF_EXPERT = D_FF // 2
N_EVEN = (DEPTH + 1) // 2
N_ODD = DEPTH // 2

kernel_name = 'hybrid_conv_dilated_deltanet_moe_block'


def rms_norm(x, g):
    xf = x.astype(jnp.float32)
    y = xf * lax.rsqrt(jnp.mean(xf * xf, axis=-1, keepdims=True) + EPS)
    return (y * g.astype(jnp.float32)).astype(x.dtype)


def l2_norm(x):
    xf = x.astype(jnp.float32)
    return xf * lax.rsqrt(jnp.sum(xf * xf, axis=-1, keepdims=True) + EPS)


def causal_dwconv(x, w):
    k = w.shape[0]
    return lax.conv_general_dilated(x, w[:, None, :], window_strides=(1,), padding=[(k - 1, 0)],
                                    dimension_numbers=('NWC', 'WIO', 'NWC'),
                                    feature_group_count=x.shape[-1])


def banded_attention(q, k, v, w):
    n, L, h, hd = q.shape
    nb = -(-L // w)
    pad = nb * w - L
    q = jnp.pad(q, ((0, 0), (0, pad), (0, 0), (0, 0))).reshape(n, nb, w, h, hd)
    k = jnp.pad(k, ((0, 0), (w, pad), (0, 0), (0, 0))).reshape(n, nb + 1, w, h, hd)
    v = jnp.pad(v, ((0, 0), (w, pad), (0, 0), (0, 0))).reshape(n, nb + 1, w, h, hd)
    k2 = jnp.concatenate([k[:, :-1], k[:, 1:]], axis=2)
    v2 = jnp.concatenate([v[:, :-1], v[:, 1:]], axis=2)
    s = jnp.einsum('nbqhd,nbkhd->nbhqk', q, k2).astype(jnp.float32) * (hd ** -0.5)
    qi = jnp.arange(w)[:, None]
    kj = jnp.arange(2 * w)[None, :]
    dist = qi + w - kj
    kpos = jnp.arange(nb)[:, None, None] * w + kj[None] - w
    valid = (dist >= 0) & (dist <= w) & (kpos >= 0)
    s = jnp.where(valid[None, :, None], s, -jnp.inf)
    m = jnp.max(s, axis=-1, keepdims=True)
    p = jnp.exp(s - m)
    den = jnp.sum(p, axis=-1)
    o = jnp.einsum('nbhqk,nbkhd->nbqhd', p, v2.astype(jnp.float32))
    o = o / jnp.moveaxis(den, 2, 3)[..., None]
    lse = m[..., 0] + jnp.log(den)
    o = o.reshape(n, nb * w, h, hd)[:, :L]
    lse = jnp.moveaxis(lse, 2, 3).reshape(n, nb * w, h)[:, :L]
    return o, lse


def dilated_attention(q, k, v):
    b, s, h, hd = q.shape
    outs, lses = [], []
    for window, d in DILATED_PATTERN:
        L = s // d
        def to_sub(t):
            return t.reshape(b, L, d, h, hd).transpose(0, 2, 1, 3, 4).reshape(b * d, L, h, hd)
        o, lse = banded_attention(to_sub(q), to_sub(k), to_sub(v), window // d)
        outs.append(o.reshape(b, d, L, h, hd).transpose(0, 2, 1, 3, 4).reshape(b, s, h, hd))
        lses.append(lse.reshape(b, d, L, h).transpose(0, 2, 1, 3).reshape(b, s, h))
    wts = jax.nn.softmax(jnp.stack(lses), axis=0)
    return jnp.einsum('gbsh,gbshd->bshd', wts, jnp.stack(outs))


def conv_dilated_mixer(u, w_in, conv_w, q_g, k_g, w_out):
    b, s, _ = u.shape
    gb, gc, xc, q, k, v = jnp.split(u @ w_in, 6, axis=-1)
    a = gb * causal_dwconv(gc * xc, conv_w)
    q = rms_norm(q.reshape(b, s, B_HEADS, B_HEAD_DIM), q_g)
    k = rms_norm(k.reshape(b, s, B_HEADS, B_HEAD_DIM), k_g)
    v = v.reshape(b, s, B_HEADS, B_HEAD_DIM)
    o = dilated_attention(q, k, v).astype(u.dtype).reshape(b, s, B_HEADS * B_HEAD_DIM)
    return jnp.concatenate([a, o], axis=-1) @ w_out


def gated_delta_rule(q, k, v, beta, g):
    b, s, h, dk = q.shape
    dv = v.shape[-1]
    nc = s // C_CHUNK
    def chunk(t):
        t = t.astype(jnp.float32).reshape((b, nc, C_CHUNK) + t.shape[2:])
        return jnp.moveaxis(t, 3, 2)
    q = chunk(l2_norm(q)) * (dk ** -0.5)
    k = chunk(l2_norm(k))
    v = chunk(v)
    beta = chunk(beta)
    gc = jnp.cumsum(chunk(g), axis=-1)
    causal = jnp.tril(jnp.ones((C_CHUNK, C_CHUNK), dtype=bool))
    strict = jnp.tril(jnp.ones((C_CHUNK, C_CHUNK), dtype=bool), -1)
    decay = jnp.exp(jnp.where(causal, gc[..., :, None] - gc[..., None, :], -jnp.inf))
    kk = jnp.einsum('bnhid,bnhjd->bnhij', k, k)
    a_mat = jnp.where(strict, beta[..., :, None] * kk * decay, 0.0) + jnp.eye(C_CHUNK, dtype=jnp.float32)
    u = lax.linalg.triangular_solve(a_mat, v * beta[..., None], left_side=True, lower=True, unit_diagonal=True)
    w = lax.linalg.triangular_solve(a_mat, k * (beta * jnp.exp(gc))[..., None], left_side=True, lower=True,
                                    unit_diagonal=True)
    attn = jnp.einsum('bnhid,bnhjd->bnhij', q, k) * decay
    q_dec = q * jnp.exp(gc)[..., None]
    k_dec = k * jnp.exp(gc[..., -1:] - gc)[..., None]
    g_last = jnp.exp(gc[..., -1])

    def step(state, inp):
        qd, kd, uc, wc, at, gl = inp
        v_new = uc - jnp.einsum('bhcd,bhde->bhce', wc, state)
        o = jnp.einsum('bhcd,bhde->bhce', qd, state) + jnp.einsum('bhij,bhje->bhie', at, v_new)
        state = state * gl[..., None, None] + jnp.einsum('bhcd,bhce->bhde', kd, v_new)
        return state, o

    xs = (jnp.moveaxis(q_dec, 1, 0), jnp.moveaxis(k_dec, 1, 0), jnp.moveaxis(u, 1, 0),
          jnp.moveaxis(w, 1, 0), jnp.moveaxis(attn, 1, 0), jnp.moveaxis(g_last, 1, 0))
    state0 = jnp.zeros((b, h, dk, dv), jnp.float32)
    _, o = lax.scan(step, state0, xs)
    return jnp.moveaxis(jnp.moveaxis(o, 0, 1), 2, 3).reshape(b, s, h, dv)


def deltanet_mixer(u, w_in, conv_w, a_log, dt_bias, o_g, w_out):
    b, s, _ = u.shape
    proj = u @ w_in
    qkv, gate, bb, aa = jnp.split(proj, [3 * D_MODEL, 4 * D_MODEL, 4 * D_MODEL + C_HEADS], axis=-1)
    qkv = jax.nn.silu(causal_dwconv(qkv, conv_w))
    q, k, v = [t.reshape(b, s, C_HEADS, C_HEAD_DIM) for t in jnp.split(qkv, 3, axis=-1)]
    beta = jax.nn.sigmoid(bb.astype(jnp.float32))
    g = -jnp.exp(a_log.astype(jnp.float32)) * jax.nn.softplus(aa.astype(jnp.float32) + dt_bias.astype(jnp.float32))
    o = gated_delta_rule(q, k, v, beta, g)
    o = rms_norm(o, o_g).astype(u.dtype) * jax.nn.silu(gate.reshape(b, s, C_HEADS, C_HEAD_DIM))
    return o.reshape(b, s, C_HEADS * C_HEAD_DIM) @ w_out


def memory_cross_attention(u, mem_n, w_q, w_kv, q_g, k_g, w_o):
    b, s, _ = u.shape
    m = mem_n.shape[1]
    q = rms_norm((u @ w_q).reshape(b, s, X_HEADS, X_HEAD_DIM), q_g)
    k, v = jnp.split(mem_n @ w_kv, 2, axis=-1)
    k = rms_norm(k.reshape(b, m, X_HEADS, X_HEAD_DIM), k_g)
    v = v.reshape(b, m, X_HEADS, X_HEAD_DIM)
    sc = jnp.einsum('bshd,bmhd->bhsm', q, k).astype(jnp.float32) * (X_HEAD_DIM ** -0.5)
    p = jax.nn.softmax(sc, axis=-1).astype(v.dtype)
    o = jnp.einsum('bhsm,bmhd->bshd', p, v).reshape(b, s, X_HEADS * X_HEAD_DIM)
    return o @ w_o


def swiglu(u, w_gu, w_down):
    gt, up = jnp.split(u @ w_gu, 2, axis=-1)
    return (jax.nn.silu(gt) * up) @ w_down


def moe_swiglu(u, router, w_gu, w_down):
    logits = (u @ router).astype(jnp.float32)
    top_val, top_idx = lax.top_k(logits, TOP_K)
    top_w = jax.nn.softmax(top_val, axis=-1)
    combine = jnp.sum(jax.nn.one_hot(top_idx, N_EXPERTS, dtype=jnp.float32) * top_w[..., None], axis=-2)
    y = jnp.zeros_like(u)
    for e in range(N_EXPERTS):
        y = y + combine[..., e:e + 1].astype(u.dtype) * swiglu(u, w_gu[e], w_down[e])
    return y


def setup_inputs(seed: int = 0) -> dict:
    key = jax.random.key(seed)
    ks = iter(jax.random.split(key, 40))
    def nrm(shape, scale):
        return jax.random.normal(next(ks), shape, jnp.float32) * scale
    def gain(shape):
        return 1.0 + 0.02 * jax.random.normal(next(ks), shape, jnp.float32)
    d = D_MODEL
    dt = jnp.exp(jax.random.uniform(next(ks), (N_ODD, C_HEADS), jnp.float32, np.log(1e-3), np.log(1e-1)))
    return {
        'x': nrm((BATCH, SEQ, d), 1.0),
        'mem': nrm((BATCH, N_MEM, d), 1.0),
        'norm_mix': gain((DEPTH, d)),
        'norm_xattn': gain((DEPTH, d)),
        'norm_mem': gain((DEPTH, d)),
        'norm_ffn': gain((DEPTH, d)),
        'ev_w_in': nrm((N_EVEN, d, 3 * A_WIDTH + 3 * B_HEADS * B_HEAD_DIM), d ** -0.5),
        'ev_conv': nrm((N_EVEN, A_CONV, A_WIDTH), A_CONV ** -0.5),
        'ev_q_norm': gain((N_EVEN, B_HEAD_DIM)),
        'ev_k_norm': gain((N_EVEN, B_HEAD_DIM)),
        'ev_w_out': nrm((N_EVEN, A_WIDTH + B_HEADS * B_HEAD_DIM, d), (A_WIDTH + B_HEADS * B_HEAD_DIM) ** -0.5),
        'od_w_in': nrm((N_ODD, d, 4 * C_HEADS * C_HEAD_DIM + 2 * C_HEADS), d ** -0.5),
        'od_conv': nrm((N_ODD, C_CONV, 3 * C_HEADS * C_HEAD_DIM), C_CONV ** -0.5),
        'od_a_log': jnp.log(jax.random.uniform(next(ks), (N_ODD, C_HEADS), jnp.float32, 1.0, 16.0)),
        'od_dt_bias': jnp.log(jnp.expm1(dt)),
        'od_o_norm': gain((N_ODD, C_HEAD_DIM)),
        'od_w_out': nrm((N_ODD, C_HEADS * C_HEAD_DIM, d), (C_HEADS * C_HEAD_DIM) ** -0.5),
        'xa_w_q': nrm((DEPTH, d, X_HEADS * X_HEAD_DIM), d ** -0.5),
        'xa_w_kv': nrm((DEPTH, d, 2 * X_HEADS * X_HEAD_DIM), d ** -0.5),
        'xa_q_norm': gain((DEPTH, X_HEAD_DIM)),
        'xa_k_norm': gain((DEPTH, X_HEAD_DIM)),
        'xa_w_o': nrm((DEPTH, X_HEADS * X_HEAD_DIM, d), (X_HEADS * X_HEAD_DIM) ** -0.5),
        'ff_w_gu': nrm((N_EVEN, d, 2 * D_FF), d ** -0.5),
        'ff_w_down': nrm((N_EVEN, D_FF, d), D_FF ** -0.5),
        'moe_router': nrm((N_ODD, d, N_EXPERTS), d ** -0.5),
        'moe_w_gu': nrm((N_ODD, N_EXPERTS, d, 2 * D_FF_EXPERT), d ** -0.5),
        'moe_w_down': nrm((N_ODD, N_EXPERTS, D_FF_EXPERT, d), D_FF_EXPERT ** -0.5),
    }


def reference(x, mem, norm_mix, norm_xattn, norm_mem, norm_ffn,
              ev_w_in, ev_conv, ev_q_norm, ev_k_norm, ev_w_out,
              od_w_in, od_conv, od_a_log, od_dt_bias, od_o_norm, od_w_out,
              xa_w_q, xa_w_kv, xa_q_norm, xa_k_norm, xa_w_o,
              ff_w_gu, ff_w_down, moe_router, moe_w_gu, moe_w_down):
    h = x
    for layer in range(DEPTH):
        i = layer // 2
        u = rms_norm(h, norm_mix[layer])
        if layer % 2 == 0:
            h = h + conv_dilated_mixer(u, ev_w_in[i], ev_conv[i], ev_q_norm[i], ev_k_norm[i], ev_w_out[i])
        else:
            h = h + deltanet_mixer(u, od_w_in[i], od_conv[i], od_a_log[i], od_dt_bias[i], od_o_norm[i], od_w_out[i])
        h = h + memory_cross_attention(rms_norm(h, norm_xattn[layer]), rms_norm(mem, norm_mem[layer]),
                                       xa_w_q[layer], xa_w_kv[layer], xa_q_norm[layer], xa_k_norm[layer],
                                       xa_w_o[layer])
        u = rms_norm(h, norm_ffn[layer])
        if layer % 2 == 0:
            h = h + swiglu(u, ff_w_gu[i], ff_w_down[i])
        else:
            h = h + moe_swiglu(u, moe_router[i], moe_w_gu[i], moe_w_down[i])
    return h
```

```python
import functools
import math

import jax
import jax.numpy as jnp
from jax import lax
from jax.experimental import pallas as pl
from jax.experimental.pallas import tpu as pltpu

F32 = jnp.float32
BF16 = jnp.bfloat16

EPS = 1e-6
D_MODEL = 1024
A_WIDTH = 512
B_HEADS = 8
B_HEAD_DIM = 64
DILATED_PATTERN = ((128, 1), (512, 4), (2048, 16))
C_HEADS = 8
C_HEAD_DIM = 128
C_CHUNK = 64
X_HEADS = 4
X_HEAD_DIM = 64
N_EXPERTS = 8

LANES = 128
VMEM_LIMIT = 56 * 1024 * 1024
NEG = -1e30


def _params(*sem):
    return pltpu.CompilerParams(dimension_semantics=sem, vmem_limit_bytes=VMEM_LIMIT)


def _resident(shape):
    nd = len(shape)
    return pl.BlockSpec(shape, lambda *_: (0,) * nd, pipeline_mode=pl.Buffered(1))


def _rms_rows(x, g):
    ms = jnp.mean(x * x, axis=-1, keepdims=True)
    return x * lax.rsqrt(ms + EPS) * g


def _silu(x):
    return x * jax.nn.sigmoid(x)


def _dot(a, b):
    return jnp.dot(a, b, preferred_element_type=F32)


def _dot_nt(a, b):
    return lax.dot_general(a, b, (((1,), (1,)), ((), ())), preferred_element_type=F32)


def _dot_tn(a, b):
    return lax.dot_general(a, b, (((0,), (0,)), ((), ())), preferred_element_type=F32)


def _head_inv_rms(x, head_dim):
    width = x.shape[-1]
    lane = lax.broadcasted_iota(jnp.int32, (1, width), 1)
    x2 = x * x
    inv = jnp.zeros_like(x)
    for h in range(width // head_dim):
        sel = (lane >= h * head_dim) & (lane < (h + 1) * head_dim)
        ms = jnp.sum(jnp.where(sel, x2, 0.0), axis=-1, keepdims=True) * (1.0 / head_dim)
        inv = jnp.where(sel, lax.rsqrt(ms + EPS), inv)
    return inv


def _norm_proj_kernel(x_ref, g_ref, w_ref, o_ref, *, chunk):
    xn = _rms_rows(x_ref[...], g_ref[...]).astype(BF16)
    for c in range(w_ref.shape[1] // chunk):
        cs = slice(c * chunk, (c + 1) * chunk)
        o_ref[:, cs] = _dot(xn, w_ref[:, cs]).astype(o_ref.dtype)


def _norm_proj_gates_kernel(x_ref, g_ref, w_ref, wg_ref, o_ref, og_ref, *, chunk):
    xf = _rms_rows(x_ref[...], g_ref[...])
    xn = xf.astype(BF16)
    for c in range(w_ref.shape[1] // chunk):
        cs = slice(c * chunk, (c + 1) * chunk)
        o_ref[:, cs] = _dot(xn, w_ref[:, cs]).astype(o_ref.dtype)
    og_ref[...] = jnp.dot(xf, wg_ref[...], preferred_element_type=F32, precision=lax.Precision.HIGHEST)


def norm_proj(x, g, w, w_gates=None, *, tm=512, chunk=512):
    n, d = x.shape
    nout = w.shape[1]
    in_specs = [pl.BlockSpec((tm, d), lambda i: (i, 0)), _resident((1, d)), _resident((d, nout))]
    if w_gates is None:
        return pl.pallas_call(
            functools.partial(_norm_proj_kernel, chunk=chunk),
            grid=(n // tm,),
            in_specs=in_specs,
            out_specs=pl.BlockSpec((tm, nout), lambda i: (i, 0)),
            out_shape=jax.ShapeDtypeStruct((n, nout), BF16),
            compiler_params=_params("parallel"),
            name="norm_proj",
        )(x, g, w)
    ng = w_gates.shape[1]
    return pl.pallas_call(
        functools.partial(_norm_proj_gates_kernel, chunk=chunk),
        grid=(n // tm,),
        in_specs=in_specs + [_resident((d, ng))],
        out_specs=[pl.BlockSpec((tm, nout), lambda i: (i, 0)), pl.BlockSpec((tm, ng), lambda i: (i, 0))],
        out_shape=[jax.ShapeDtypeStruct((n, nout), BF16), jax.ShapeDtypeStruct((n, ng), F32)],
        compiler_params=_params("parallel"),
        name="norm_proj_gates",
    )(x, g, w, w_gates)


def _dilation_bias(seq, blk):
    nd = seq // blk
    r = jnp.arange(blk)[:, None]
    c = jnp.arange(blk)[None, :]
    dist = (jnp.arange(nd)[:, None, None] * blk) + r - c
    mult = jnp.zeros(dist.shape, F32)
    for window, d in DILATED_PATTERN:
        mult = mult + ((dist >= 0) & (dist % d == 0) & (dist <= window)).astype(F32)
    return jnp.where(mult > 0, jnp.log(jnp.maximum(mult, 1.0)), NEG)


def _ev_mix_kernel(gb_ref, gc_ref, xc_ref, q_ref, k_ref, v_ref, cw_ref, qg_ref, kg_ref, bias_ref,
                   a_ref, o_ref, kn_ref, *, blk):
    seq = q_ref.shape[1]
    y = gc_ref[0].astype(F32) * xc_ref[0].astype(F32)
    row = lax.broadcasted_iota(jnp.int32, (seq, 1), 0)
    kw = cw_ref.shape[0]
    conv = y * cw_ref[kw - 1:kw, :]
    for sh in range(1, kw):
        ys = jnp.where(row >= sh, pltpu.roll(y, sh, axis=0), 0.0)
        conv = conv + ys * cw_ref[kw - 1 - sh:kw - sh, :]
    a_ref[0] = (gb_ref[0].astype(F32) * conv).astype(a_ref.dtype)

    lane = lax.broadcasted_iota(jnp.int32, (1, LANES), 1)
    first = lane < B_HEAD_DIM
    kf = k_ref[0].astype(F32)
    kn_ref[...] = (kf * _head_inv_rms(kf, B_HEAD_DIM) * kg_ref[...]).astype(BF16)
    scale = B_HEAD_DIM ** -0.5
    for qi in range(seq // blk):
        qf = q_ref[0, qi * blk:(qi + 1) * blk, :].astype(F32)
        qn = qf * _head_inv_rms(qf, B_HEAD_DIM) * (qg_ref[...] * scale)
        outs = []
        for h in range(2):
            qh = jnp.where(first if h == 0 else jnp.logical_not(first), qn, 0.0).astype(BF16)

            def body(jj, carry, qh=qh, qi=qi):
                m, l, acc = carry
                j = qi - jj
                ks = pl.ds(pl.multiple_of(j * blk, blk), blk)
                s = _dot_nt(qh, kn_ref[ks, :]) + bias_ref[jj]
                m_new = jnp.maximum(m, jnp.max(s, axis=-1, keepdims=True))
                alpha = jnp.exp(m - m_new)
                p = jnp.exp(s - m_new)
                l = alpha * l + jnp.sum(p, axis=-1, keepdims=True)
                acc = alpha * acc + _dot(p.astype(BF16), v_ref[0, ks, :])
                return m_new, l, acc

            init = (jnp.full((blk, 1), NEG, F32), jnp.zeros((blk, 1), F32), jnp.zeros((blk, LANES), F32))
            _, l, acc = lax.fori_loop(0, qi + 1, body, init)
            outs.append(acc / l)
        o_ref[0, qi * blk:(qi + 1) * blk, :] = jnp.where(first, outs[0], outs[1]).astype(o_ref.dtype)


def ev_mix(proj, conv_w, q_g, k_g, *, batch, seq, blk=256):
    nslab = A_WIDTH // LANES
    bias = _dilation_bias(seq, blk)
    qg2 = jnp.tile(q_g.reshape(1, -1), (1, LANES // B_HEAD_DIM))
    kg2 = jnp.tile(k_g.reshape(1, -1), (1, LANES // B_HEAD_DIM))

    def slab(off):
        return pl.BlockSpec((1, seq, LANES), lambda b, s: (b, 0, off * nslab + s))

    out_spec = pl.BlockSpec((1, seq, LANES), lambda b, s: (b, 0, s))
    return pl.pallas_call(
        functools.partial(_ev_mix_kernel, blk=blk),
        grid=(batch, nslab),
        in_specs=[slab(0), slab(1), slab(2), slab(3), slab(4), slab(5),
                  pl.BlockSpec((conv_w.shape[0], LANES), lambda b, s: (0, s)),
                  _resident((1, LANES)), _resident((1, LANES)), _resident(bias.shape)],
        out_specs=[out_spec, out_spec],
        out_shape=[jax.ShapeDtypeStruct((batch, seq, A_WIDTH), BF16)] * 2,
        scratch_shapes=[pltpu.VMEM((seq, LANES), BF16)],
        compiler_params=_params("parallel", "parallel"),
        name="ev_mix",
    )(proj, proj, proj, proj, proj, proj, conv_w, qg2, kg2, bias)


def _proj_res_kernel(*refs, n_in):
    h_ref = refs[0]
    o_ref = refs[-1]
    acc = h_ref[...]
    for i in range(n_in):
        acc = acc + _dot(refs[1 + i][...], refs[1 + n_in + i][...])
    o_ref[...] = acc


def proj_res(h, xs, ws, *, tm=512):
    n, d = h.shape
    in_specs = [pl.BlockSpec((tm, d), lambda i: (i, 0))]
    in_specs += [pl.BlockSpec((tm, x.shape[1]), lambda i: (i, 0)) for x in xs]
    in_specs += [_resident(w.shape) for w in ws]
    return pl.pallas_call(
        functools.partial(_proj_res_kernel, n_in=len(xs)),
        grid=(n // tm,),
        in_specs=in_specs,
        out_specs=pl.BlockSpec((tm, d), lambda i: (i, 0)),
        out_shape=jax.ShapeDtypeStruct((n, d), F32),
        compiler_params=_params("parallel"),
        name="proj_res",
    )(h, *xs, *ws)


def _mem_kv_kernel(m_ref, g_ref, w_ref, kg_ref, o_ref):
    width = X_HEADS * X_HEAD_DIM
    mn = _rms_rows(m_ref[0], g_ref[...]).astype(BF16)
    kv = _dot(mn, w_ref[...])
    k = kv[:, :width]
    o_ref[0, :, :width] = (k * _head_inv_rms(k, X_HEAD_DIM) * kg_ref[...]).astype(o_ref.dtype)
    o_ref[0, :, width:] = kv[:, width:].astype(o_ref.dtype)


def mem_kv(mem, g, w_kv, k_g):
    b, m, d = mem.shape
    width = w_kv.shape[1]
    kg = jnp.tile(k_g.reshape(1, -1), (1, X_HEADS))
    return pl.pallas_call(
        _mem_kv_kernel,
        grid=(b,),
        in_specs=[pl.BlockSpec((1, m, d), lambda i: (i, 0, 0)), _resident((1, d)), _resident(w_kv.shape),
                  _resident(kg.shape)],
        out_specs=pl.BlockSpec((1, m, width), lambda i: (i, 0, 0)),
        out_shape=jax.ShapeDtypeStruct((b, m, width), BF16),
        compiler_params=_params("parallel"),
        name="mem_kv",
    )(mem, g, w_kv, kg)


def _xattn_kernel(h_ref, g_ref, wq_ref, qg_ref, kv_ref, wo_ref, o_ref):
    width = X_HEADS * X_HEAD_DIM
    x = h_ref[...]
    u = _rms_rows(x, g_ref[...]).astype(BF16)
    q = _dot(u, wq_ref[...])
    q = q * _head_inv_rms(q, X_HEAD_DIM) * (qg_ref[...] * X_HEAD_DIM ** -0.5)
    k = kv_ref[0, :, :width]
    v = kv_ref[0, :, width:]
    lane = lax.broadcasted_iota(jnp.int32, (1, width), 1)
    out = jnp.zeros(q.shape, F32)
    for h in range(X_HEADS):
        sel = (lane >= h * X_HEAD_DIM) & (lane < (h + 1) * X_HEAD_DIM)
        s = _dot_nt(jnp.where(sel, q, 0.0).astype(BF16), k)
        p = jnp.exp(s - jnp.max(s, axis=-1, keepdims=True))
        oh = _dot(p.astype(BF16), v) / jnp.sum(p, axis=-1, keepdims=True)
        out = jnp.where(sel, oh, out)
    o_ref[...] = x + _dot(out.astype(BF16), wo_ref[...])


def xattn(h, g, w_q, q_g, kv, w_o, *, seq, tm=512):
    n, d = h.shape
    qg = jnp.tile(q_g.reshape(1, -1), (1, X_HEADS))
    per_seq = seq // tm
    return pl.pallas_call(
        _xattn_kernel,
        grid=(n // tm,),
        in_specs=[pl.BlockSpec((tm, d), lambda i: (i, 0)), _resident((1, d)), _resident(w_q.shape),
                  _resident(qg.shape),
                  pl.BlockSpec((1,) + kv.shape[1:], lambda i: (i // per_seq, 0, 0)),
                  _resident(w_o.shape)],
        out_specs=pl.BlockSpec((tm, d), lambda i: (i, 0)),
        out_shape=jax.ShapeDtypeStruct((n, d), F32),
        compiler_params=_params("parallel"),
        name="xattn",
    )(h, g, w_q, qg, kv, w_o)


def _swiglu_act(u, wgu_ref, act_ref, d_ff, chunk):
    for c in range(d_ff // chunk):
        gate = _dot(u, wgu_ref[:, c * chunk:(c + 1) * chunk])
        up = _dot(u, wgu_ref[:, d_ff + c * chunk:d_ff + (c + 1) * chunk])
        act_ref[:, c * chunk:(c + 1) * chunk] = (_silu(gate) * up).astype(act_ref.dtype)


def _ffn_kernel(h_ref, g_ref, wgu_ref, wd_ref, o_ref, act_ref, *, chunk):
    x = h_ref[...]
    u = _rms_rows(x, g_ref[...]).astype(BF16)
    _swiglu_act(u, wgu_ref, act_ref, wd_ref.shape[0], chunk)
    o_ref[...] = x + _dot(act_ref[...], wd_ref[...])


def ffn(h, g, w_gu, w_down, *, tm=512, chunk=256):
    n, d = h.shape
    d_ff = w_down.shape[0]
    return pl.pallas_call(
        functools.partial(_ffn_kernel, chunk=chunk),
        grid=(n // tm,),
        in_specs=[pl.BlockSpec((tm, d), lambda i: (i, 0)), _resident((1, d)), _resident(w_gu.shape),
                  _resident(w_down.shape)],
        out_specs=pl.BlockSpec((tm, d), lambda i: (i, 0)),
        out_shape=jax.ShapeDtypeStruct((n, d), F32),
        scratch_shapes=[pltpu.VMEM((tm, d_ff), BF16)],
        compiler_params=_params("parallel"),
        name="ffn",
    )(h, g, w_gu, w_down)


def _moe_kernel(h_ref, g_ref, r_ref, wgu_ref, wd_ref, o_ref, u_ref, comb_ref, act_ref, *, chunk):
    e = pl.program_id(1)
    lane = lax.broadcasted_iota(jnp.int32, (1, LANES), 1)

    @pl.when(e == 0)
    def _():
        x = h_ref[...]
        uf = _rms_rows(x, g_ref[...])
        u_ref[...] = uf.astype(BF16)
        logits = jnp.dot(uf, r_ref[...], preferred_element_type=F32, precision=lax.Precision.HIGHEST)
        logits = jnp.where(lane < N_EXPERTS, logits, NEG)
        m1 = jnp.max(logits, axis=-1, keepdims=True)
        i1 = jnp.min(jnp.where(logits == m1, lane, LANES), axis=-1, keepdims=True)
        rest = jnp.where(lane == i1, NEG, logits)
        m2 = jnp.max(rest, axis=-1, keepdims=True)
        i2 = jnp.min(jnp.where(rest == m2, lane, LANES), axis=-1, keepdims=True)
        e2 = jnp.exp(m2 - m1)
        w1 = 1.0 / (1.0 + e2)
        comb_ref[...] = jnp.where(lane == i1, w1, 0.0) + jnp.where(lane == i2, e2 * w1, 0.0)
        o_ref[...] = x

    _swiglu_act(u_ref[...], wgu_ref.at[0], act_ref, wd_ref.shape[1], chunk)
    y = _dot(act_ref[...], wd_ref[0])
    ce = jnp.sum(jnp.where(lane == e, comb_ref[...], 0.0), axis=-1, keepdims=True)
    o_ref[...] += ce * y


def moe(h, g, router, w_gu, w_down, *, tm=1024, chunk=128):
    n, d = h.shape
    n_exp, d_ff, _ = w_down.shape
    r_pad = jnp.pad(router, ((0, 0), (0, LANES - n_exp)))
    return pl.pallas_call(
        functools.partial(_moe_kernel, chunk=chunk),
        grid=(n // tm, n_exp),
        in_specs=[pl.BlockSpec((tm, d), lambda i, e: (i, 0)), _resident((1, d)), _resident(r_pad.shape),
                  pl.BlockSpec((1, d, 2 * d_ff), lambda i, e: (e, 0, 0)),
                  pl.BlockSpec((1, d_ff, d), lambda i, e: (e, 0, 0))],
        out_specs=pl.BlockSpec((tm, d), lambda i, e: (i, 0)),
        out_shape=jax.ShapeDtypeStruct((n, d), F32),
        scratch_shapes=[pltpu.VMEM((tm, d), BF16), pltpu.VMEM((tm, LANES), F32), pltpu.VMEM((tm, d_ff), BF16)],
        compiler_params=_params("parallel", "arbitrary"),
        name="moe",
    )(h, g, r_pad, w_gu, w_down)


def _pair_mask(ii, jj, s):
    bi = ii // s
    return (bi % 2 == 1) & (jj // s == bi - 1)


def _delta_kernel(q_ref, k_ref, v_ref, gate_ref, ba_ref, cq_ref, ck_ref, cv_ref, alog_ref, dtb_ref, og_ref,
                  o_ref, qs, ks, vs, us, ws, ats, beta_s, g_s, glast_s):
    seq = q_ref.shape[1]
    ck = C_CHUNK
    head = pl.program_id(1)
    row = lax.broadcasted_iota(jnp.int32, (seq, 1), 0)
    lane = lax.broadcasted_iota(jnp.int32, (1, LANES), 1)

    def conv_silu(x_ref, w_ref):
        x = x_ref[0].astype(F32)
        kw = w_ref.shape[0]
        y = x * w_ref[kw - 1:kw, :]
        for sh in range(1, kw):
            y = y + jnp.where(row >= sh, pltpu.roll(x, sh, axis=0), 0.0) * w_ref[kw - 1 - sh:kw - sh, :]
        return _silu(y)

    def l2n(x):
        return x * lax.rsqrt(jnp.sum(x * x, axis=-1, keepdims=True) + EPS)

    qs[...] = l2n(conv_silu(q_ref, cq_ref)) * (C_HEAD_DIM ** -0.5)
    ks[...] = l2n(conv_silu(k_ref, ck_ref))
    vs[...] = conv_silu(v_ref, cv_ref)

    ba = ba_ref[0]
    beta_all = jax.nn.sigmoid(ba)
    sp = jnp.maximum(ba + dtb_ref[...], 0.0) + jnp.log(1.0 + jnp.exp(-jnp.abs(ba + dtb_ref[...])))
    g_all = -jnp.exp(alog_ref[...]) * sp
    beta_s[...] = jnp.broadcast_to(jnp.sum(jnp.where(lane == head, beta_all, 0.0), axis=-1, keepdims=True),
                                   (seq, LANES))
    g_s[...] = jnp.broadcast_to(jnp.sum(jnp.where(lane == head + C_HEADS, g_all, 0.0), axis=-1, keepdims=True),
                                (seq, LANES))

    r_c = lax.broadcasted_iota(jnp.int32, (ck, LANES), 0)
    ii = lax.broadcasted_iota(jnp.int32, (ck, ck), 0)
    jj = lax.broadcasted_iota(jnp.int32, (ck, ck), 1)
    eye = (ii == jj).astype(F32)

    def prep(c, carry):
        sl = pl.ds(pl.multiple_of(c * ck, ck), ck)
        qc, kc, vc = qs[sl, :], ks[sl, :], vs[sl, :]
        beta = beta_s[sl, :]
        gcs = g_s[sl, :]
        sh = 1
        while sh < ck:
            gcs = gcs + jnp.where(r_c >= sh, pltpu.roll(gcs, sh, axis=0), 0.0)
            sh *= 2
        egc = jnp.exp(gcs)
        glast = gcs[ck - 1:ck, :]
        gc_row = jnp.transpose(gcs)[:ck, :]
        decay = jnp.exp(jnp.where(ii >= jj, gcs[:, :ck] - gc_row, NEG))
        a_mat = jnp.where(ii > jj, beta[:, :ck] * _dot_nt(kc, kc) * decay, 0.0)
        t = eye - jnp.where(_pair_mask(ii, jj, 1), a_mat, 0.0)
        s = 2
        while s < ck:
            t = t - _dot(_dot(t, jnp.where(_pair_mask(ii, jj, s), a_mat, 0.0)), t)
            s *= 2
        us[sl, :] = _dot(t, vc * beta)
        ws[sl, :] = _dot(t, kc * (beta * egc))
        ats[sl, :] = _dot_nt(qc, kc) * decay
        qs[sl, :] = qc * egc
        ks[sl, :] = kc * jnp.exp(glast - gcs)
        glast_s[pl.ds(c, 1), :] = jnp.exp(glast)
        return carry

    lax.fori_loop(0, seq // ck, prep, 0)

    def scan(c, state):
        sl = pl.ds(pl.multiple_of(c * ck, ck), ck)
        v_new = us[sl, :] - _dot(ws[sl, :], state)
        o = _dot(qs[sl, :], state) + _dot(ats[sl, :], v_new)
        state = state * glast_s[pl.ds(c, 1), :] + _dot_tn(ks[sl, :], v_new)
        gate = gate_ref[0, sl, :].astype(F32)
        o_ref[0, sl, :] = (_rms_rows(o, og_ref[...]) * _silu(gate)).astype(o_ref.dtype)
        return state

    lax.fori_loop(0, seq // ck, scan, jnp.zeros((C_HEAD_DIM, C_HEAD_DIM), F32))


def delta_mix(proj, ba, conv_w, a_log, dt_bias, o_g, *, batch, seq):
    nh = C_HEADS
    alog = jnp.pad(a_log.reshape(1, nh), ((0, 0), (nh, LANES - 2 * nh)))
    dtb = jnp.pad(dt_bias.reshape(1, nh), ((0, 0), (nh, LANES - 2 * nh)))

    def slab(off):
        return pl.BlockSpec((1, seq, LANES), lambda b, h: (b, 0, off * nh + h))

    def conv_slab(off):
        return pl.BlockSpec((conv_w.shape[0], LANES), lambda b, h: (0, off * nh + h))

    vec = pltpu.VMEM((seq, LANES), F32)
    return pl.pallas_call(
        _delta_kernel,
        grid=(batch, nh),
        in_specs=[slab(0), slab(1), slab(2), slab(3),
                  pl.BlockSpec((1, seq, LANES), lambda b, h: (b, 0, 0)),
                  conv_slab(0), conv_slab(1), conv_slab(2),
                  _resident((1, LANES)), _resident((1, LANES)), _resident((1, LANES))],
        out_specs=pl.BlockSpec((1, seq, LANES), lambda b, h: (b, 0, h)),
        out_shape=jax.ShapeDtypeStruct((batch, seq, nh * LANES), BF16),
        scratch_shapes=[vec, vec, vec, vec, vec, pltpu.VMEM((seq, C_CHUNK), F32), vec, vec,
                        pltpu.VMEM((seq // C_CHUNK, LANES), F32)],
        compiler_params=_params("parallel", "parallel"),
        name="delta_mix",
    )(proj, proj, proj, proj, ba, conv_w, conv_w, conv_w, alog, dtb, o_g.reshape(1, LANES))


def kernel(x, mem, norm_mix, norm_xattn, norm_mem, norm_ffn, ev_w_in, ev_conv, ev_q_norm, ev_k_norm, ev_w_out, od_w_in, od_conv, od_a_log, od_dt_bias, od_o_norm, od_w_out, xa_w_q, xa_w_kv, xa_q_norm, xa_k_norm, xa_w_o, ff_w_gu, ff_w_down, moe_router, moe_w_gu, moe_w_down):
    batch, seq, d = x.shape
    n = batch * seq
    h = x.reshape(n, d)
    depth = norm_mix.shape[0]
    for layer in range(depth):
        i = layer // 2
        g_mix = norm_mix[layer].reshape(1, d)
        if layer % 2 == 0:
            proj = norm_proj(h, g_mix, ev_w_in[i].astype(BF16))
            a, o = ev_mix(proj.reshape(batch, seq, -1), ev_conv[i], ev_q_norm[i], ev_k_norm[i],
                          batch=batch, seq=seq)
            w_out = ev_w_out[i].astype(BF16)
            h = proj_res(h, [a.reshape(n, -1), o.reshape(n, -1)], [w_out[:A_WIDTH], w_out[A_WIDTH:]])
        else:
            n_main = 4 * C_HEADS * C_HEAD_DIM
            w_gates = jnp.pad(od_w_in[i][:, n_main:], ((0, 0), (0, LANES - 2 * C_HEADS)))
            proj, ba = norm_proj(h, g_mix, od_w_in[i][:, :n_main].astype(BF16), w_gates)
            o = delta_mix(proj.reshape(batch, seq, -1), ba.reshape(batch, seq, -1), od_conv[i], od_a_log[i],
                          od_dt_bias[i], od_o_norm[i], batch=batch, seq=seq)
            h = proj_res(h, [o.reshape(n, -1)], [od_w_out[i].astype(BF16)])
        kv = mem_kv(mem, norm_mem[layer].reshape(1, d), xa_w_kv[layer].astype(BF16), xa_k_norm[layer])
        h = xattn(h, norm_xattn[layer].reshape(1, d), xa_w_q[layer].astype(BF16), xa_q_norm[layer], kv,
                  xa_w_o[layer].astype(BF16), seq=seq)
        g_ffn = norm_ffn[layer].reshape(1, d)
        if layer % 2 == 0:
            h = ffn(h, g_ffn, ff_w_gu[i].astype(BF16), ff_w_down[i].astype(BF16))
        else:
            h = moe(h, g_ffn, moe_router[i], moe_w_gu[i].astype(BF16), moe_w_down[i].astype(BF16))
    return h.reshape(batch, seq, d)
```

```python
import functools

import jax
import jax.numpy as jnp
from jax import lax
from jax.experimental import pallas as pl
from jax.experimental.pallas import tpu as pltpu

F32 = jnp.float32
BF16 = jnp.bfloat16

EPS = 1e-6
D_MODEL = 1024
A_WIDTH = 512
B_HEADS = 8
B_HEAD_DIM = 64
DILATED_PATTERN = ((128, 1), (512, 4), (2048, 16))
C_HEADS = 8
C_HEAD_DIM = 128
C_CHUNK = 64
X_HEADS = 4
X_HEAD_DIM = 64
N_EXPERTS = 8

LANES = 128
VMEM_LIMIT = 56 * 1024 * 1024
NEG = -1e30


def _params(*sem):
    return pltpu.CompilerParams(dimension_semantics=sem, vmem_limit_bytes=VMEM_LIMIT)


def _resident(shape):
    nd = len(shape)
    return pl.BlockSpec(shape, lambda *_: (0,) * nd, pipeline_mode=pl.Buffered(1))


def _rms_rows(x, g):
    ms = jnp.mean(x * x, axis=-1, keepdims=True)
    return x * lax.rsqrt(ms + EPS) * g


def _silu(x):
    return x * jax.nn.sigmoid(x)


def _dot(a, b):
    return jnp.dot(a, b, preferred_element_type=F32)


def _dot_nt(a, b):
    return lax.dot_general(a, b, (((1,), (1,)), ((), ())), preferred_element_type=F32)


def _head_inv_rms(x, head_dim):
    width = x.shape[-1]
    lane = lax.broadcasted_iota(jnp.int32, (1, width), 1)
    x2 = x * x
    inv = jnp.zeros_like(x)
    for h in range(width // head_dim):
        sel = (lane >= h * head_dim) & (lane < (h + 1) * head_dim)
        ms = jnp.sum(jnp.where(sel, x2, 0.0), axis=-1, keepdims=True) * (1.0 / head_dim)
        inv = jnp.where(sel, lax.rsqrt(ms + EPS), inv)
    return inv


def _norm_proj_kernel(x_ref, g_ref, w_ref, o_ref, *, chunk):
    xn = _rms_rows(x_ref[...], g_ref[...]).astype(BF16)
    for c in range(w_ref.shape[1] // chunk):
        cs = slice(c * chunk, (c + 1) * chunk)
        o_ref[:, cs] = _dot(xn, w_ref[:, cs]).astype(o_ref.dtype)


def _norm_proj_gates_kernel(x_ref, g_ref, w_ref, wg_ref, o_ref, og_ref, *, chunk):
    xf = _rms_rows(x_ref[...], g_ref[...])
    xn = xf.astype(BF16)
    for c in range(w_ref.shape[1] // chunk):
        cs = slice(c * chunk, (c + 1) * chunk)
        o_ref[:, cs] = _dot(xn, w_ref[:, cs]).astype(o_ref.dtype)
    og_ref[...] = jnp.dot(xf, wg_ref[...], preferred_element_type=F32, precision=lax.Precision.HIGHEST)


def norm_proj(x, g, w, w_gates=None, *, tm=512, chunk=512):
    n, d = x.shape
    nout = w.shape[1]
    in_specs = [pl.BlockSpec((tm, d), lambda i: (i, 0)), _resident((1, d)), _resident((d, nout))]
    if w_gates is None:
        return pl.pallas_call(
            functools.partial(_norm_proj_kernel, chunk=chunk),
            grid=(n // tm,),
            in_specs=in_specs,
            out_specs=pl.BlockSpec((tm, nout), lambda i: (i, 0)),
            out_shape=jax.ShapeDtypeStruct((n, nout), BF16),
            compiler_params=_params("parallel"),
            name="norm_proj",
        )(x, g, w)
    ng = w_gates.shape[1]
    return pl.pallas_call(
        functools.partial(_norm_proj_gates_kernel, chunk=chunk),
        grid=(n // tm,),
        in_specs=in_specs + [_resident((d, ng))],
        out_specs=[pl.BlockSpec((tm, nout), lambda i: (i, 0)), pl.BlockSpec((tm, ng), lambda i: (i, 0))],
        out_shape=[jax.ShapeDtypeStruct((n, nout), BF16), jax.ShapeDtypeStruct((n, ng), F32)],
        compiler_params=_params("parallel"),
        name="norm_proj_gates",
    )(x, g, w, w_gates)


def _dilation_bias(seq, blk):
    dist = (seq - blk) + jnp.arange(blk)[:, None] - jnp.arange(seq)[None, :]
    mult = jnp.zeros(dist.shape, F32)
    for window, d in DILATED_PATTERN:
        mult = mult + ((dist >= 0) & (dist % d == 0) & (dist <= window)).astype(F32)
    return jnp.where(mult > 0, jnp.log(jnp.maximum(mult, 1.0)), NEG)


def _ev_mix_kernel(gb_ref, gc_ref, xc_ref, q_ref, k_ref, v_ref, cw_ref, qg_ref, kg_ref, bias_ref,
                   a_ref, o_ref, kn_ref, *, blk):
    seq = q_ref.shape[1]
    y = gc_ref[0].astype(F32) * xc_ref[0].astype(F32)
    row = lax.broadcasted_iota(jnp.int32, (seq, 1), 0)
    kw = cw_ref.shape[0]
    conv = y * cw_ref[kw - 1:kw, :]
    for sh in range(1, kw):
        ys = jnp.where(row >= sh, pltpu.roll(y, sh, axis=0), 0.0)
        conv = conv + ys * cw_ref[kw - 1 - sh:kw - sh, :]
    a_ref[0] = (gb_ref[0].astype(F32) * conv).astype(a_ref.dtype)

    lane = lax.broadcasted_iota(jnp.int32, (1, LANES), 1)
    first = lane < B_HEAD_DIM
    kf = k_ref[0].astype(F32)
    kn_ref[...] = (kf * _head_inv_rms(kf, B_HEAD_DIM) * kg_ref[...]).astype(BF16)
    scale = B_HEAD_DIM ** -0.5
    for qi in range(seq // blk):
        qf = q_ref[0, qi * blk:(qi + 1) * blk, :].astype(F32)
        qn = qf * _head_inv_rms(qf, B_HEAD_DIM) * (qg_ref[...] * scale)
        kv_len = (qi + 1) * blk
        bias = bias_ref[:, seq - kv_len:]
        outs = []
        for h in range(2):
            qh = jnp.where(first if h == 0 else jnp.logical_not(first), qn, 0.0).astype(BF16)
            s = _dot_nt(qh, kn_ref[:kv_len, :]) + bias
            p = jnp.exp(s - jnp.max(s, axis=-1, keepdims=True))
            l = jnp.sum(p, axis=-1, keepdims=True)
            outs.append(_dot(p.astype(BF16), v_ref[0, :kv_len, :]) / l)
        o_ref[0, qi * blk:(qi + 1) * blk, :] = jnp.where(first, outs[0], outs[1]).astype(o_ref.dtype)


def ev_mix(proj, conv_w, q_g, k_g, *, batch, seq, blk=256):
    nslab = A_WIDTH // LANES
    bias = _dilation_bias(seq, blk)
    qg2 = jnp.tile(q_g.reshape(1, -1), (1, LANES // B_HEAD_DIM))
    kg2 = jnp.tile(k_g.reshape(1, -1), (1, LANES // B_HEAD_DIM))

    def slab(off):
        return pl.BlockSpec((1, seq, LANES), lambda b, s: (b, 0, off * nslab + s))

    out_spec = pl.BlockSpec((1, seq, LANES), lambda b, s: (b, 0, s))
    return pl.pallas_call(
        functools.partial(_ev_mix_kernel, blk=blk),
        grid=(batch, nslab),
        in_specs=[slab(0), slab(1), slab(2), slab(3), slab(4), slab(5),
                  pl.BlockSpec((conv_w.shape[0], LANES), lambda b, s: (0, s)),
                  _resident((1, LANES)), _resident((1, LANES)), _resident(bias.shape)],
        out_specs=[out_spec, out_spec],
        out_shape=[jax.ShapeDtypeStruct((batch, seq, A_WIDTH), BF16)] * 2,
        scratch_shapes=[pltpu.VMEM((seq, LANES), BF16)],
        compiler_params=_params("parallel", "parallel"),
        name="ev_mix",
    )(proj, proj, proj, proj, proj, proj, conv_w, qg2, kg2, bias)


def _proj_res_kernel(*refs, n_in):
    h_ref = refs[0]
    o_ref = refs[-1]
    acc = h_ref[...]
    for i in range(n_in):
        acc = acc + _dot(refs[1 + i][...], refs[1 + n_in + i][...])
    o_ref[...] = acc


def proj_res(h, xs, ws, *, tm=512):
    n, d = h.shape
    in_specs = [pl.BlockSpec((tm, d), lambda i: (i, 0))]
    in_specs += [pl.BlockSpec((tm, x.shape[1]), lambda i: (i, 0)) for x in xs]
    in_specs += [_resident(w.shape) for w in ws]
    return pl.pallas_call(
        functools.partial(_proj_res_kernel, n_in=len(xs)),
        grid=(n // tm,),
        in_specs=in_specs,
        out_specs=pl.BlockSpec((tm, d), lambda i: (i, 0)),
        out_shape=jax.ShapeDtypeStruct((n, d), F32),
        compiler_params=_params("parallel"),
        name="proj_res",
    )(h, *xs, *ws)


def _mem_kv_kernel(m_ref, g_ref, w_ref, kg_ref, o_ref):
    width = X_HEADS * X_HEAD_DIM
    mn = _rms_rows(m_ref[0], g_ref[...]).astype(BF16)
    kv = _dot(mn, w_ref[...])
    k = kv[:, :width]
    o_ref[0, :, :width] = (k * _head_inv_rms(k, X_HEAD_DIM) * kg_ref[...]).astype(o_ref.dtype)
    o_ref[0, :, width:] = kv[:, width:].astype(o_ref.dtype)


def mem_kv(mem, g, w_kv, k_g):
    b, m, d = mem.shape
    width = w_kv.shape[1]
    kg = jnp.tile(k_g.reshape(1, -1), (1, X_HEADS))
    return pl.pallas_call(
        _mem_kv_kernel,
        grid=(b,),
        in_specs=[pl.BlockSpec((1, m, d), lambda i: (i, 0, 0)), _resident((1, d)), _resident(w_kv.shape),
                  _resident(kg.shape)],
        out_specs=pl.BlockSpec((1, m, width), lambda i: (i, 0, 0)),
        out_shape=jax.ShapeDtypeStruct((b, m, width), BF16),
        compiler_params=_params("parallel"),
        name="mem_kv",
    )(mem, g, w_kv, kg)


def _xattn_kernel(h_ref, g_ref, wq_ref, qg_ref, kv_ref, wo_ref, o_ref):
    width = X_HEADS * X_HEAD_DIM
    x = h_ref[...]
    u = _rms_rows(x, g_ref[...]).astype(BF16)
    q = _dot(u, wq_ref[...])
    q = q * _head_inv_rms(q, X_HEAD_DIM) * (qg_ref[...] * X_HEAD_DIM ** -0.5)
    k = kv_ref[0, :, :width]
    v = kv_ref[0, :, width:]
    lane = lax.broadcasted_iota(jnp.int32, (1, width), 1)
    out = jnp.zeros(q.shape, F32)
    for h in range(X_HEADS):
        sel = (lane >= h * X_HEAD_DIM) & (lane < (h + 1) * X_HEAD_DIM)
        s = _dot_nt(jnp.where(sel, q, 0.0).astype(BF16), k)
        p = jnp.exp(s - jnp.max(s, axis=-1, keepdims=True))
        oh = _dot(p.astype(BF16), v) / jnp.sum(p, axis=-1, keepdims=True)
        out = jnp.where(sel, oh, out)
    o_ref[...] = x + _dot(out.astype(BF16), wo_ref[...])


def xattn(h, g, w_q, q_g, kv, w_o, *, seq, tm=512):
    n, d = h.shape
    qg = jnp.tile(q_g.reshape(1, -1), (1, X_HEADS))
    per_seq = seq // tm
    return pl.pallas_call(
        _xattn_kernel,
        grid=(n // tm,),
        in_specs=[pl.BlockSpec((tm, d), lambda i: (i, 0)), _resident((1, d)), _resident(w_q.shape),
                  _resident(qg.shape),
                  pl.BlockSpec((1,) + kv.shape[1:], lambda i: (i // per_seq, 0, 0)),
                  _resident(w_o.shape)],
        out_specs=pl.BlockSpec((tm, d), lambda i: (i, 0)),
        out_shape=jax.ShapeDtypeStruct((n, d), F32),
        compiler_params=_params("parallel"),
        name="xattn",
    )(h, g, w_q, qg, kv, w_o)


def _swiglu_act(u, wgu_ref, act_ref, d_ff, chunk):
    for lo in range(0, d_ff, chunk):
        hi = min(lo + chunk, d_ff)
        gate = _dot(u, wgu_ref[:, lo:hi])
        up = _dot(u, wgu_ref[:, d_ff + lo:d_ff + hi])
        act_ref[:, lo:hi] = (_silu(gate) * up).astype(act_ref.dtype)


def _ffn_kernel(h_ref, g_ref, wgu_ref, wd_ref, o_ref, act_ref, *, chunk):
    x = h_ref[...]
    u = _rms_rows(x, g_ref[...]).astype(BF16)
    _swiglu_act(u, wgu_ref, act_ref, wd_ref.shape[0], chunk)
    o_ref[...] = x + _dot(act_ref[...], wd_ref[...])


def ffn(h, g, w_gu, w_down, *, tm=512, chunk=256):
    n, d = h.shape
    d_ff = w_down.shape[0]
    return pl.pallas_call(
        functools.partial(_ffn_kernel, chunk=chunk),
        grid=(n // tm,),
        in_specs=[pl.BlockSpec((tm, d), lambda i: (i, 0)), _resident((1, d)), _resident(w_gu.shape),
                  _resident(w_down.shape)],
        out_specs=pl.BlockSpec((tm, d), lambda i: (i, 0)),
        out_shape=jax.ShapeDtypeStruct((n, d), F32),
        scratch_shapes=[pltpu.VMEM((tm, d_ff), BF16)],
        compiler_params=_params("parallel"),
        name="ffn",
    )(h, g, w_gu, w_down)


def _moe_kernel(h_ref, g_ref, r_ref, wgu_ref, wd_ref, o_ref, u_ref, comb_ref, act_ref, *, chunk):
    e = pl.program_id(1)
    lane = lax.broadcasted_iota(jnp.int32, (1, LANES), 1)

    @pl.when(e == 0)
    def _():
        x = h_ref[...]
        uf = _rms_rows(x, g_ref[...])
        u_ref[...] = uf.astype(BF16)
        logits = jnp.dot(uf, r_ref[...], preferred_element_type=F32, precision=lax.Precision.HIGHEST)
        logits = jnp.where(lane < N_EXPERTS, logits, NEG)
        m1 = jnp.max(logits, axis=-1, keepdims=True)
        i1 = jnp.min(jnp.where(logits == m1, lane, LANES), axis=-1, keepdims=True)
        rest = jnp.where(lane == i1, NEG, logits)
        m2 = jnp.max(rest, axis=-1, keepdims=True)
        i2 = jnp.min(jnp.where(rest == m2, lane, LANES), axis=-1, keepdims=True)
        e2 = jnp.exp(m2 - m1)
        w1 = 1.0 / (1.0 + e2)
        comb_ref[...] = jnp.where(lane == i1, w1, 0.0) + jnp.where(lane == i2, e2 * w1, 0.0)
        o_ref[...] = x

    _swiglu_act(u_ref[...], wgu_ref.at[0], act_ref, wd_ref.shape[1], chunk)
    y = _dot(act_ref[...], wd_ref[0])
    ce = jnp.sum(jnp.where(lane == e, comb_ref[...], 0.0), axis=-1, keepdims=True)
    o_ref[...] += ce * y


def moe(h, g, router, w_gu, w_down, *, tm=1024, chunk=512):
    n, d = h.shape
    n_exp, d_ff, _ = w_down.shape
    r_pad = jnp.pad(router, ((0, 0), (0, LANES - n_exp)))
    return pl.pallas_call(
        functools.partial(_moe_kernel, chunk=chunk),
        grid=(n // tm, n_exp),
        in_specs=[pl.BlockSpec((tm, d), lambda i, e: (i, 0)), _resident((1, d)), _resident(r_pad.shape),
                  pl.BlockSpec((1, d, 2 * d_ff), lambda i, e: (e, 0, 0)),
                  pl.BlockSpec((1, d_ff, d), lambda i, e: (e, 0, 0))],
        out_specs=pl.BlockSpec((tm, d), lambda i, e: (i, 0)),
        out_shape=jax.ShapeDtypeStruct((n, d), F32),
        scratch_shapes=[pltpu.VMEM((tm, d), BF16), pltpu.VMEM((tm, LANES), F32), pltpu.VMEM((tm, d_ff), BF16)],
        compiler_params=_params("parallel", "arbitrary"),
        name="moe",
    )(h, g, r_pad, w_gu, w_down)


def _pair_mask(ii, jj, s):
    bi = ii // s
    return (bi % 2 == 1) & (jj // s == bi - 1)


def _bmm(a, b):
    return lax.dot_general(a, b, (((2,), (1,)), ((0,), (0,))), preferred_element_type=F32)


def _bmm_nt(a, b):
    return lax.dot_general(a, b, (((2,), (2,)), ((0,), (0,))), preferred_element_type=F32)


def _delta_kernel(q_ref, k_ref, v_ref, gate_ref, ba_ref, cq_ref, ck_ref, cv_ref, alog_ref, dtb_ref, og_ref,
                  o_ref, p_s, n_s, egl_s, sall_s, *, heads):
    seq = q_ref.shape[1]
    ck = C_CHUNK
    nc = seq // ck
    dh = C_HEAD_DIM
    group = pl.program_id(1)
    row = lax.broadcasted_iota(jnp.int32, (seq, 1), 0)
    rc = row % ck
    lane = lax.broadcasted_iota(jnp.int32, (1, LANES), 1)
    ii = lax.broadcasted_iota(jnp.int32, (1, ck, ck), 1)
    jj = lax.broadcasted_iota(jnp.int32, (1, ck, ck), 2)
    eye = (ii == jj).astype(F32)

    ba = ba_ref[0]
    beta_all = jax.nn.sigmoid(ba)
    z = ba + dtb_ref[...]
    g_all = -jnp.exp(alog_ref[...]) * (jnp.maximum(z, 0.0) + jnp.log(1.0 + jnp.exp(-jnp.abs(z))))

    def conv_silu(x, w_ref, ls):
        kw = w_ref.shape[0]
        y = x * w_ref[kw - 1:kw, ls]
        for sh in range(1, kw):
            y = y + jnp.where(row >= sh, pltpu.roll(x, sh, axis=0), 0.0) * w_ref[kw - 1 - sh:kw - sh, ls]
        return _silu(y)

    def l2n(x):
        return x * lax.rsqrt(jnp.sum(x * x, axis=-1, keepdims=True) + EPS)

    prepared = []
    for g in range(heads):
        ls = slice(g * dh, (g + 1) * dh)
        head = group * heads + g
        q = l2n(conv_silu(q_ref[0, :, ls].astype(F32), cq_ref, ls)) * (dh ** -0.5)
        k = l2n(conv_silu(k_ref[0, :, ls].astype(F32), ck_ref, ls))
        v = conv_silu(v_ref[0, :, ls].astype(F32), cv_ref, ls)
        beta = jnp.broadcast_to(jnp.sum(jnp.where(lane == head, beta_all, 0.0), axis=-1, keepdims=True),
                                (seq, LANES))
        gcs = jnp.broadcast_to(jnp.sum(jnp.where(lane == head + C_HEADS, g_all, 0.0), axis=-1, keepdims=True),
                               (seq, LANES))
        sh = 1
        while sh < ck:
            gcs = gcs + jnp.where(rc >= sh, pltpu.roll(gcs, sh, axis=0), 0.0)
            sh *= 2
        q3, k3, v3 = (t.reshape(nc, ck, dh) for t in (q, k, v))
        beta3 = beta.reshape(nc, ck, LANES)
        gcs3 = gcs.reshape(nc, ck, LANES)
        egc3 = jnp.exp(gcs3)
        glast = gcs3[:, ck - 1:ck, :]
        gc_row = jnp.swapaxes(gcs3, 1, 2)[:, :ck, :]
        decay = jnp.exp(jnp.where(ii >= jj, gcs3[:, :, :ck] - gc_row, NEG))
        kb = k3.astype(BF16)
        qk = _bmm_nt(jnp.concatenate([q3.astype(BF16), kb], axis=1), kb)
        attn = (qk[:, :ck] * decay).astype(BF16)
        a_mat = jnp.where(ii > jj, beta3[:, :, :ck] * qk[:, ck:] * decay, 0.0)
        t = eye - jnp.where(_pair_mask(ii, jj, 1), a_mat, 0.0)
        s = 2
        while s < ck:
            tb = t.astype(BF16)
            x = _bmm(tb, jnp.where(_pair_mask(ii, jj, s), a_mat, 0.0).astype(BF16))
            t = t - _bmm(x.astype(BF16), tb)
            s *= 2
        rhs = jnp.concatenate([v3 * beta3, k3 * (beta3 * egc3)], axis=2).astype(BF16)
        uw = _bmm(t.astype(BF16), rhs)
        kd_t = jnp.swapaxes(k3 * jnp.exp(glast - gcs3), 1, 2).astype(BF16)
        pn = _bmm(kd_t, uw.astype(BF16))
        n_s[g] = pn[:, :, :dh]
        p_s[g] = pn[:, :, dh:].astype(BF16)
        egl_s[g] = jnp.exp(glast)
        wq = jnp.concatenate([uw[:, :, dh:], q3 * egc3], axis=1).astype(BF16)
        prepared.append((wq, attn, uw[:, :, :dh]))

    def scan(c, states):
        new = []
        for g in range(heads):
            sb = states[g].astype(BF16)
            sall_s[g, c] = sb
            new.append(states[g] * egl_s[g, c] + n_s[g, c] - _dot(p_s[g, c], sb))
        return tuple(new)

    lax.fori_loop(0, nc, scan, tuple(jnp.zeros((dh, dh), F32) for _ in range(heads)))

    for g in range(heads):
        ls = slice(g * dh, (g + 1) * dh)
        wq, attn, u = prepared[g]
        r = _bmm(wq, sall_s[g])
        v_new = u - r[:, :ck]
        o = (r[:, ck:] + _bmm(attn, v_new.astype(BF16))).reshape(seq, dh)
        gate = gate_ref[0, :, ls].astype(F32)
        o_ref[0, :, ls] = (_rms_rows(o, og_ref[...]) * _silu(gate)).astype(o_ref.dtype)


def delta_mix(proj, ba, conv_w, a_log, dt_bias, o_g, *, batch, seq, heads=2):
    nh = C_HEADS
    ngroup = nh // heads
    width = heads * C_HEAD_DIM
    nc = seq // C_CHUNK
    alog = jnp.pad(a_log.reshape(1, nh), ((0, 0), (nh, LANES - 2 * nh)))
    dtb = jnp.pad(dt_bias.reshape(1, nh), ((0, 0), (nh, LANES - 2 * nh)))

    def slab(off):
        return pl.BlockSpec((1, seq, width), lambda b, h: (b, 0, off * ngroup + h))

    def conv_slab(off):
        return pl.BlockSpec((conv_w.shape[0], width), lambda b, h: (0, off * ngroup + h))

    return pl.pallas_call(
        functools.partial(_delta_kernel, heads=heads),
        grid=(batch, ngroup),
        in_specs=[slab(0), slab(1), slab(2), slab(3),
                  pl.BlockSpec((1, seq, LANES), lambda b, h: (b, 0, 0)),
                  conv_slab(0), conv_slab(1), conv_slab(2),
                  _resident((1, LANES)), _resident((1, LANES)), _resident((1, LANES))],
        out_specs=pl.BlockSpec((1, seq, width), lambda b, h: (b, 0, h)),
        out_shape=jax.ShapeDtypeStruct((batch, seq, nh * C_HEAD_DIM), BF16),
        scratch_shapes=[pltpu.VMEM((heads, nc, C_HEAD_DIM, C_HEAD_DIM), BF16),
                        pltpu.VMEM((heads, nc, C_HEAD_DIM, C_HEAD_DIM), F32),
                        pltpu.VMEM((heads, nc, 1, LANES), F32),
                        pltpu.VMEM((heads, nc, C_HEAD_DIM, C_HEAD_DIM), BF16)],
        compiler_params=_params("parallel", "parallel"),
        name="delta_mix",
    )(proj, proj, proj, proj, ba, conv_w, conv_w, conv_w, alog, dtb, o_g.reshape(1, LANES))


def kernel(x, mem, norm_mix, norm_xattn, norm_mem, norm_ffn, ev_w_in, ev_conv, ev_q_norm, ev_k_norm, ev_w_out, od_w_in, od_conv, od_a_log, od_dt_bias, od_o_norm, od_w_out, xa_w_q, xa_w_kv, xa_q_norm, xa_k_norm, xa_w_o, ff_w_gu, ff_w_down, moe_router, moe_w_gu, moe_w_down):
    batch, seq, d = x.shape
    n = batch * seq
    h = x.reshape(n, d)
    depth = norm_mix.shape[0]
    for layer in range(depth):
        i = layer // 2
        g_mix = norm_mix[layer].reshape(1, d)
        if layer % 2 == 0:
            proj = norm_proj(h, g_mix, ev_w_in[i].astype(BF16))
            a, o = ev_mix(proj.reshape(batch, seq, -1), ev_conv[i], ev_q_norm[i], ev_k_norm[i],
                          batch=batch, seq=seq)
            w_out = ev_w_out[i].astype(BF16)
            h = proj_res(h, [a.reshape(n, -1), o.reshape(n, -1)], [w_out[:A_WIDTH], w_out[A_WIDTH:]])
        else:
            n_main = 4 * C_HEADS * C_HEAD_DIM
            w_gates = jnp.pad(od_w_in[i][:, n_main:], ((0, 0), (0, LANES - 2 * C_HEADS)))
            proj, ba = norm_proj(h, g_mix, od_w_in[i][:, :n_main].astype(BF16), w_gates)
            o = delta_mix(proj.reshape(batch, seq, -1), ba.reshape(batch, seq, -1), od_conv[i], od_a_log[i],
                          od_dt_bias[i], od_o_norm[i], batch=batch, seq=seq)
            h = proj_res(h, [o.reshape(n, -1)], [od_w_out[i].astype(BF16)])
        kv = mem_kv(mem, norm_mem[layer].reshape(1, d), xa_w_kv[layer].astype(BF16), xa_k_norm[layer])
        h = xattn(h, norm_xattn[layer].reshape(1, d), xa_w_q[layer].astype(BF16), xa_q_norm[layer], kv,
                  xa_w_o[layer].astype(BF16), seq=seq)
        g_ffn = norm_ffn[layer].reshape(1, d)
        if layer % 2 == 0:
            h = ffn(h, g_ffn, ff_w_gu[i].astype(BF16), ff_w_down[i].astype(BF16))
        else:
            h = moe(h, g_ffn, moe_router[i], moe_w_gu[i].astype(BF16), moe_w_down[i].astype(BF16))
    return h.reshape(batch, seq, d)
```

```python
import functools

import jax
import jax.numpy as jnp
from jax import lax
from jax.experimental import pallas as pl
from jax.experimental.pallas import tpu as pltpu

F32 = jnp.float32
BF16 = jnp.bfloat16

EPS = 1e-6
D_MODEL = 1024
A_WIDTH = 512
B_HEADS = 8
B_HEAD_DIM = 64
DILATED_PATTERN = ((128, 1), (512, 4), (2048, 16))
C_HEADS = 8
C_HEAD_DIM = 128
C_CHUNK = 64
X_HEADS = 4
X_HEAD_DIM = 64
N_EXPERTS = 8

LANES = 128
SUBLANES = 8
VMEM_LIMIT = 56 * 1024 * 1024
NEG = -1e30


def _params(*sem):
    return pltpu.CompilerParams(dimension_semantics=sem, vmem_limit_bytes=VMEM_LIMIT)


def _resident(shape):
    nd = len(shape)
    return pl.BlockSpec(shape, lambda *_: (0,) * nd, pipeline_mode=pl.Buffered(1))


def _rms_rows(x, g):
    ms = jnp.mean(x * x, axis=-1, keepdims=True)
    return x * lax.rsqrt(ms + EPS) * g


def _silu(x):
    return x * jax.nn.sigmoid(x)


def _dot(a, b):
    return jnp.dot(a, b, preferred_element_type=F32)


def _dot_nt(a, b):
    return lax.dot_general(a, b, (((1,), (1,)), ((), ())), preferred_element_type=F32)


def _head_inv_rms(x, head_dim):
    width = x.shape[-1]
    lane = lax.broadcasted_iota(jnp.int32, (1, width), 1)
    x2 = x * x
    inv = jnp.zeros_like(x)
    for h in range(width // head_dim):
        sel = (lane >= h * head_dim) & (lane < (h + 1) * head_dim)
        ms = jnp.sum(jnp.where(sel, x2, 0.0), axis=-1, keepdims=True) * (1.0 / head_dim)
        inv = jnp.where(sel, lax.rsqrt(ms + EPS), inv)
    return inv


def _norm_proj_kernel(x_ref, g_ref, w_ref, o_ref, *, chunk):
    xn = _rms_rows(x_ref[...], g_ref[...]).astype(BF16)
    for c in range(w_ref.shape[1] // chunk):
        cs = slice(c * chunk, (c + 1) * chunk)
        o_ref[:, cs] = _dot(xn, w_ref[:, cs]).astype(o_ref.dtype)


def _norm_proj_gates_kernel(x_ref, g_ref, w_ref, wg_ref, alog_ref, dtb_ref, o_ref, og_ref, *, chunk):
    xf = _rms_rows(x_ref[...], g_ref[...])
    xn = xf.astype(BF16)
    for c in range(w_ref.shape[1] // chunk):
        cs = slice(c * chunk, (c + 1) * chunk)
        o_ref[:, cs] = _dot(xn, w_ref[:, cs]).astype(o_ref.dtype)
    ba = _dot_split(xf, wg_ref)
    lane = lax.broadcasted_iota(jnp.int32, (1, LANES), 1)
    z = ba + dtb_ref[...]
    softplus = jnp.maximum(z, 0.0) + jnp.log(1.0 + jnp.exp(-jnp.abs(z)))
    og_ref[...] = jnp.where(lane < C_HEADS, jax.nn.sigmoid(ba), -jnp.exp(alog_ref[...]) * softplus)


def norm_proj(x, g, w, w_gates=None, a_log=None, dt_bias=None, *, tm=512, chunk=512):
    n, d = x.shape
    nout = w.shape[1]
    in_specs = [pl.BlockSpec((tm, d), lambda i: (i, 0)), _resident((1, d)), _resident((d, nout))]
    if w_gates is None:
        return pl.pallas_call(
            functools.partial(_norm_proj_kernel, chunk=chunk),
            grid=(n // tm,),
            in_specs=in_specs,
            out_specs=pl.BlockSpec((tm, nout), lambda i: (i, 0)),
            out_shape=jax.ShapeDtypeStruct((n, nout), BF16),
            compiler_params=_params("parallel"),
            name="norm_proj",
        )(x, g, w)
    ng = w_gates.shape[1] // 2
    nh = a_log.shape[0]
    alog = jnp.pad(a_log.reshape(1, nh), ((0, 0), (nh, ng - 2 * nh)))
    dtb = jnp.pad(dt_bias.reshape(1, nh), ((0, 0), (nh, ng - 2 * nh)))
    return pl.pallas_call(
        functools.partial(_norm_proj_gates_kernel, chunk=chunk),
        grid=(n // tm,),
        in_specs=in_specs + [_resident(w_gates.shape), _resident((1, ng)), _resident((1, ng))],
        out_specs=[pl.BlockSpec((tm, nout), lambda i: (i, 0)), pl.BlockSpec((tm, ng), lambda i: (i, 0))],
        out_shape=[jax.ShapeDtypeStruct((n, nout), BF16), jax.ShapeDtypeStruct((n, ng), F32)],
        compiler_params=_params("parallel"),
        name="norm_proj_gates",
    )(x, g, w, w_gates, alog, dtb)


def _dilation_bias(seq, blk):
    dist = (seq - blk) + jnp.arange(blk)[:, None] - jnp.arange(seq)[None, :]
    mult = jnp.zeros(dist.shape, F32)
    for window, d in DILATED_PATTERN:
        mult = mult + ((dist >= 0) & (dist % d == 0) & (dist <= window)).astype(F32)
    return jnp.where(mult > 0, jnp.log(jnp.maximum(mult, 1.0)), NEG)


def _ev_mix_kernel(gb_ref, gc_ref, xc_ref, q_ref, k_ref, v_ref, cw_ref, qg_ref, kg_ref, bias_ref,
                   a_ref, o_ref, kn_ref, *, blk):
    seq = q_ref.shape[1]
    y = gc_ref[0].astype(F32) * xc_ref[0].astype(F32)
    row = lax.broadcasted_iota(jnp.int32, (seq, 1), 0)
    kw = cw_ref.shape[0]
    conv = y * cw_ref[kw - 1:kw, :]
    for sh in range(1, kw):
        ys = jnp.where(row >= sh, pltpu.roll(y, sh, axis=0), 0.0)
        conv = conv + ys * cw_ref[kw - 1 - sh:kw - sh, :]
    a_ref[0] = (gb_ref[0].astype(F32) * conv).astype(a_ref.dtype)

    lane = lax.broadcasted_iota(jnp.int32, (1, LANES), 1)
    first = lane < B_HEAD_DIM
    kf = k_ref[0].astype(F32)
    kn_ref[...] = (kf * _head_inv_rms(kf, B_HEAD_DIM) * kg_ref[...]).astype(BF16)
    scale = B_HEAD_DIM ** -0.5
    for qi in range(seq // blk):
        qf = q_ref[0, qi * blk:(qi + 1) * blk, :].astype(F32)
        qn = qf * _head_inv_rms(qf, B_HEAD_DIM) * (qg_ref[...] * scale)
        kv_len = (qi + 1) * blk
        bias = bias_ref[:, seq - kv_len:]
        outs = []
        for h in range(2):
            qh = jnp.where(first if h == 0 else jnp.logical_not(first), qn, 0.0).astype(BF16)
            s = _dot_nt(qh, kn_ref[:kv_len, :]) + bias
            p = jnp.exp(s - jnp.max(s, axis=-1, keepdims=True))
            l = jnp.sum(p, axis=-1, keepdims=True)
            outs.append(_dot(p.astype(BF16), v_ref[0, :kv_len, :]) / l)
        o_ref[0, qi * blk:(qi + 1) * blk, :] = jnp.where(first, outs[0], outs[1]).astype(o_ref.dtype)


def ev_mix(proj, conv_w, q_g, k_g, *, batch, seq, blk=256):
    nslab = A_WIDTH // LANES
    bias = _dilation_bias(seq, blk)
    qg2 = jnp.tile(q_g.reshape(1, -1), (1, LANES // B_HEAD_DIM))
    kg2 = jnp.tile(k_g.reshape(1, -1), (1, LANES // B_HEAD_DIM))

    def slab(off):
        return pl.BlockSpec((1, seq, LANES), lambda b, s: (b, 0, off * nslab + s))

    out_spec = pl.BlockSpec((1, seq, LANES), lambda b, s: (b, 0, s))
    return pl.pallas_call(
        functools.partial(_ev_mix_kernel, blk=blk),
        grid=(batch, nslab),
        in_specs=[slab(0), slab(1), slab(2), slab(3), slab(4), slab(5),
                  pl.BlockSpec((conv_w.shape[0], LANES), lambda b, s: (0, s)),
                  _resident((1, LANES)), _resident((1, LANES)), _resident(bias.shape)],
        out_specs=[out_spec, out_spec],
        out_shape=[jax.ShapeDtypeStruct((batch, seq, A_WIDTH), BF16)] * 2,
        scratch_shapes=[pltpu.VMEM((seq, LANES), BF16)],
        compiler_params=_params("parallel", "parallel"),
        name="ev_mix",
    )(proj, proj, proj, proj, proj, proj, conv_w, qg2, kg2, bias)


def _mem_kv_kernel(m_ref, g_ref, w_ref, kg_ref, o_ref):
    width = X_HEADS * X_HEAD_DIM
    mn = _rms_rows(m_ref[0], g_ref[...]).astype(BF16)
    kv = _dot(mn, w_ref[...])
    k = kv[:, :width]
    o_ref[0, :, :width] = (k * _head_inv_rms(k, X_HEAD_DIM) * kg_ref[...]).astype(o_ref.dtype)
    o_ref[0, :, width:] = kv[:, width:].astype(o_ref.dtype)


def mem_kv(mem, g, w_kv, k_g):
    b, m, d = mem.shape
    width = w_kv.shape[1]
    kg = jnp.tile(k_g.reshape(1, -1), (1, X_HEADS))
    return pl.pallas_call(
        _mem_kv_kernel,
        grid=(b,),
        in_specs=[pl.BlockSpec((1, m, d), lambda i: (i, 0, 0)), _resident((1, d)), _resident(w_kv.shape),
                  _resident(kg.shape)],
        out_specs=pl.BlockSpec((1, m, width), lambda i: (i, 0, 0)),
        out_shape=jax.ShapeDtypeStruct((b, m, width), BF16),
        compiler_params=_params("parallel"),
        name="mem_kv",
    )(mem, g, w_kv, kg)


def _xattn_update(x, g_ref, wq_ref, qg_ref, kv_ref, wo_ref):
    width = X_HEADS * X_HEAD_DIM
    u = _rms_rows(x, g_ref[...]).astype(BF16)
    q = _dot(u, wq_ref[...])
    q = q * _head_inv_rms(q, X_HEAD_DIM) * (qg_ref[...] * X_HEAD_DIM ** -0.5)
    k = kv_ref[0, :, :width]
    v = kv_ref[0, :, width:]
    lane = lax.broadcasted_iota(jnp.int32, (1, width), 1)
    out = jnp.zeros(q.shape, F32)
    for h in range(X_HEADS):
        sel = (lane >= h * X_HEAD_DIM) & (lane < (h + 1) * X_HEAD_DIM)
        s = _dot_nt(jnp.where(sel, q, 0.0).astype(BF16), k)
        p = jnp.exp(s - jnp.max(s, axis=-1, keepdims=True))
        oh = _dot(p.astype(BF16), v) / jnp.sum(p, axis=-1, keepdims=True)
        out = jnp.where(sel, oh, out)
    return x + _dot(out.astype(BF16), wo_ref[...])


def _swiglu_act(u, wgu_ref, act_ref, d_ff, chunk):
    for lo in range(0, d_ff, chunk):
        hi = min(lo + chunk, d_ff)
        gate = _dot(u, wgu_ref[:, lo:hi])
        up = _dot(u, wgu_ref[:, d_ff + lo:d_ff + hi])
        act_ref[:, lo:hi] = (_silu(gate) * up).astype(act_ref.dtype)


def _tail_kernel(*refs, n_in, with_ffn, chunk):
    h_ref = refs[0]
    xs = refs[1:1 + n_in]
    ws = refs[1 + n_in:1 + 2 * n_in]
    xattn_refs = refs[1 + 2 * n_in:6 + 2 * n_in]
    x = h_ref[...]
    for x_ref, w_ref in zip(xs, ws):
        x = x + _dot(x_ref[...], w_ref[...])
    x = _xattn_update(x, *xattn_refs)
    if with_ffn:
        gf_ref, wgu_ref, wd_ref, o_ref, act_ref = refs[6 + 2 * n_in:]
        u = _rms_rows(x, gf_ref[...]).astype(BF16)
        _swiglu_act(u, wgu_ref, act_ref, wd_ref.shape[0], chunk)
        x = x + _dot(act_ref[...], wd_ref[...])
    else:
        o_ref = refs[-1]
    o_ref[...] = x


def layer_tail(h, xs, ws, g_x, w_q, q_g, kv, w_o, ffn_weights=None, *, seq, tm=512, chunk=256):
    n, d = h.shape
    qg = jnp.tile(q_g.reshape(1, -1), (1, X_HEADS))
    per_seq = seq // tm
    row = lambda i: (i, 0)
    in_specs = [pl.BlockSpec((tm, d), row)]
    in_specs += [pl.BlockSpec((tm, x.shape[1]), row) for x in xs]
    in_specs += [_resident(w.shape) for w in ws]
    in_specs += [_resident((1, d)), _resident(w_q.shape), _resident(qg.shape),
                 pl.BlockSpec((1,) + kv.shape[1:], lambda i: (i // per_seq, 0, 0)), _resident(w_o.shape)]
    args = [h, *xs, *ws, g_x, w_q, qg, kv, w_o]
    scratch = []
    if ffn_weights is not None:
        g_f, w_gu, w_down = ffn_weights
        in_specs += [_resident((1, d)), _resident(w_gu.shape), _resident(w_down.shape)]
        args += [g_f, w_gu, w_down]
        scratch = [pltpu.VMEM((tm, w_down.shape[0]), BF16)]
    return pl.pallas_call(
        functools.partial(_tail_kernel, n_in=len(xs), with_ffn=ffn_weights is not None, chunk=chunk),
        grid=(n // tm,),
        in_specs=in_specs,
        out_specs=pl.BlockSpec((tm, d), row),
        out_shape=jax.ShapeDtypeStruct((n, d), F32),
        scratch_shapes=scratch,
        compiler_params=_params("parallel"),
        name="layer_tail",
    )(*args)


def _split_bf16(x):
    hi = x.astype(BF16)
    return hi, (x - hi.astype(F32)).astype(BF16)


def _dot_split(x, w2_ref):
    half = w2_ref.shape[1] // 2
    xh, xl = _split_bf16(x)
    both = _dot(xh, w2_ref[...])
    return both[:, :half] + both[:, half:] + _dot(xl, w2_ref[:, :half])


def _split_weight(w):
    w = jnp.pad(w, ((0, 0), (0, LANES - w.shape[1])))
    hi = w.astype(BF16)
    return jnp.concatenate([hi, (w - hi.astype(F32)).astype(BF16)], axis=1)


def _moe_kernel(h_ref, g_ref, r2_ref, tri_ref, wgu_ref, wd_ref, o_ref,
                u_ref, wts_ref, rank_ref, rank_t_ref, act_ref, cnt_ref, *, chunk, rb, sub_t):
    e = pl.program_id(1)
    n_sub = h_ref.shape[0] // sub_t
    d_ff = wd_ref.shape[1]
    lane = lax.broadcasted_iota(jnp.int32, (1, LANES), 1)

    @pl.when(e == 0)
    def _():
        x = h_ref[...]
        uf = _rms_rows(x, g_ref[...])
        u_ref[...] = uf.astype(BF16)
        logits = _dot_split(uf, r2_ref)
        logits = jnp.where(lane < N_EXPERTS, logits, NEG)
        m1 = jnp.max(logits, axis=-1, keepdims=True)
        i1 = jnp.min(jnp.where(logits == m1, lane, LANES), axis=-1, keepdims=True)
        rest = jnp.where(lane == i1, NEG, logits)
        m2 = jnp.max(rest, axis=-1, keepdims=True)
        i2 = jnp.min(jnp.where(rest == m2, lane, LANES), axis=-1, keepdims=True)
        e2 = jnp.exp(m2 - m1)
        w1 = 1.0 / (1.0 + e2)
        wts_ref[...] = jnp.where(lane == i1, w1, 0.0) + jnp.where(lane == i2, e2 * w1, 0.0)
        sel = jnp.where(lane == i1, 1.0, jnp.where(lane == i2, 1.0, 0.0))
        for s in range(n_sub):
            sel_s = sel[s * sub_t:(s + 1) * sub_t]
            rank = jnp.where(sel_s > 0.0, _dot(tri_ref[...], sel_s.astype(BF16)), -1.0)
            rank_ref[s * sub_t:(s + 1) * sub_t, :] = rank
            rank_t_ref[s] = jnp.transpose(rank)
            counts = jnp.sum(sel_s, axis=0, keepdims=True)
            for ee in range(N_EXPERTS):
                cnt_ref[s * N_EXPERTS + ee] = jnp.sum(jnp.where(lane == ee, counts, 0.0)).astype(jnp.int32)
        o_ref[...] = x

    sub = lax.broadcasted_iota(jnp.int32, (rb, 1), 0).astype(F32)
    lan = lax.broadcasted_iota(jnp.int32, (1, rb), 1).astype(F32)
    for s in range(n_sub):
        rs = slice(s * sub_t, (s + 1) * sub_t)
        rank_col = jnp.sum(jnp.where(lane == e, rank_ref[rs, :], 0.0), axis=-1, keepdims=True)
        w_col = jnp.sum(jnp.where(lane == e, wts_ref[rs, :], 0.0), axis=-1, keepdims=True)
        rank_row = rank_t_ref[s, pl.ds(e, 1), :]

        def block(b, carry, rs=rs, rank_col=rank_col, w_col=w_col, rank_row=rank_row):
            base = (b * rb).astype(F32)
            pick = jnp.where(rank_row == base + sub, 1.0, 0.0).astype(BF16)
            xg = _dot(pick, u_ref[rs, :]).astype(BF16)
            _swiglu_act(xg, wgu_ref.at[0], act_ref, d_ff, chunk)
            y = _dot(act_ref[...], wd_ref[0]).astype(BF16)
            place = jnp.where(rank_col == base + lan, 1.0, 0.0).astype(BF16)
            o_ref[rs, :] += w_col * _dot(place, y)
            return carry

        lax.fori_loop(0, (cnt_ref[s * N_EXPERTS + e] + rb - 1) // rb, block, 0)


def moe(h, g, router, w_gu, w_down, *, tm=1024, sub_t=512, chunk=512, rb=144):
    n, d = h.shape
    n_exp, d_ff, _ = w_down.shape
    n_sub = tm // sub_t
    tri = jnp.tril(jnp.ones((sub_t, sub_t), BF16), -1)
    r2 = _split_weight(router)
    return pl.pallas_call(
        functools.partial(_moe_kernel, chunk=chunk, rb=rb, sub_t=sub_t),
        grid=(n // tm, n_exp),
        in_specs=[pl.BlockSpec((tm, d), lambda i, e: (i, 0)), _resident((1, d)), _resident(r2.shape),
                  _resident(tri.shape),
                  pl.BlockSpec((1, d, 2 * d_ff), lambda i, e: (e, 0, 0)),
                  pl.BlockSpec((1, d_ff, d), lambda i, e: (e, 0, 0))],
        out_specs=pl.BlockSpec((tm, d), lambda i, e: (i, 0)),
        out_shape=jax.ShapeDtypeStruct((n, d), F32),
        scratch_shapes=[pltpu.VMEM((tm, d), BF16), pltpu.VMEM((tm, LANES), F32), pltpu.VMEM((tm, LANES), F32),
                        pltpu.VMEM((n_sub, LANES, sub_t), F32), pltpu.VMEM((rb, d_ff), BF16),
                        pltpu.SMEM((n_sub * N_EXPERTS,), jnp.int32)],
        compiler_params=_params("parallel", "arbitrary"),
        name="moe",
    )(h, g, r2, tri, w_gu, w_down)


def _pair_mask(ii, jj, s):
    bi = ii // s
    return (bi % 2 == 1) & (jj // s == bi - 1)


def _bmm(a, b):
    return lax.dot_general(a, b, (((2,), (1,)), ((0,), (0,))), preferred_element_type=F32)


def _bmm_nt(a, b):
    return lax.dot_general(a, b, (((2,), (2,)), ((0,), (0,))), preferred_element_type=F32)


def _delta_kernel(q_ref, k_ref, v_ref, gate_ref, ba_ref, cq_ref, ck_ref, cv_ref, og_ref,
                  o_ref, p_s, n_s, egl_s, sall_s, xpad_s, *, heads):
    seq = q_ref.shape[1]
    ck = C_CHUNK
    nc = seq // ck
    dh = C_HEAD_DIM
    group = pl.program_id(1)
    row = lax.broadcasted_iota(jnp.int32, (seq, 1), 0)
    rc = row % ck
    lane = lax.broadcasted_iota(jnp.int32, (1, LANES), 1)
    ii = lax.broadcasted_iota(jnp.int32, (1, ck, ck), 1)
    jj = lax.broadcasted_iota(jnp.int32, (1, ck, ck), 2)
    eye = (ii == jj).astype(F32)

    beta_all = ba_ref[0]
    g_all = beta_all

    pad = xpad_s.shape[0] - seq
    xpad_s[0:pad, :] = jnp.zeros((pad, dh), F32)

    def conv_silu(x, w_ref, ls):
        kw = w_ref.shape[0]
        xpad_s[pad:, :] = x
        y = x * w_ref[kw - 1:kw, ls]
        for sh in range(1, kw):
            y = y + xpad_s[pad - sh:pad - sh + seq, :] * w_ref[kw - 1 - sh:kw - sh, ls]
        return _silu(y)

    def l2n(x):
        return x * lax.rsqrt(jnp.sum(x * x, axis=-1, keepdims=True) + EPS)

    prepared = []
    for g in range(heads):
        ls = slice(g * dh, (g + 1) * dh)
        head = group * heads + g
        q = l2n(conv_silu(q_ref[0, :, ls].astype(F32), cq_ref, ls)) * (dh ** -0.5)
        k = l2n(conv_silu(k_ref[0, :, ls].astype(F32), ck_ref, ls))
        v = conv_silu(v_ref[0, :, ls].astype(F32), cv_ref, ls)
        beta = jnp.broadcast_to(jnp.sum(jnp.where(lane == head, beta_all, 0.0), axis=-1, keepdims=True),
                                (seq, LANES))
        gcs = jnp.broadcast_to(jnp.sum(jnp.where(lane == head + C_HEADS, g_all, 0.0), axis=-1, keepdims=True),
                               (seq, LANES))
        sh = 1
        while sh < ck:
            gcs = gcs + jnp.where(rc >= sh, pltpu.roll(gcs, sh, axis=0), 0.0)
            sh *= 2
        q3, k3, v3 = (t.reshape(nc, ck, dh) for t in (q, k, v))
        beta3 = beta.reshape(nc, ck, LANES)
        gcs3 = gcs.reshape(nc, ck, LANES)
        egc3 = jnp.exp(gcs3)
        glast = gcs3[:, ck - 1:ck, :]
        gc_row = jnp.swapaxes(gcs3, 1, 2)[:, :ck, :]
        decay = jnp.exp(jnp.where(ii >= jj, gcs3[:, :, :ck] - gc_row, NEG))
        kb = k3.astype(BF16)
        qk = _bmm_nt(jnp.concatenate([q3.astype(BF16), kb], axis=1), kb)
        attn = (qk[:, :ck] * decay).astype(BF16)
        a_mat = jnp.where(ii > jj, beta3[:, :, :ck] * qk[:, ck:] * decay, 0.0)
        t = eye - jnp.where(_pair_mask(ii, jj, 1), a_mat, 0.0)
        s = 2
        while s < ck:
            tb = t.astype(BF16)
            x = _bmm(tb, jnp.where(_pair_mask(ii, jj, s), a_mat, 0.0).astype(BF16))
            t = t - _bmm(x.astype(BF16), tb)
            s *= 2
        rhs = jnp.concatenate([v3 * beta3, k3 * (beta3 * egc3)], axis=2).astype(BF16)
        uw = _bmm(t.astype(BF16), rhs)
        kd_t = jnp.swapaxes(k3 * jnp.exp(glast - gcs3), 1, 2).astype(BF16)
        pn = _bmm(kd_t, uw.astype(BF16))
        n_s[g] = pn[:, :, :dh]
        p_s[g] = pn[:, :, dh:].astype(BF16)
        egl_s[g] = jnp.exp(glast)
        wq = jnp.concatenate([uw[:, :, dh:], q3 * egc3], axis=1).astype(BF16)
        prepared.append((wq, attn, uw[:, :, :dh]))

    def scan(c, states):
        new = []
        for g in range(heads):
            sb = states[g].astype(BF16)
            sall_s[g, c] = sb
            new.append(states[g] * egl_s[g, c] + n_s[g, c] - _dot(p_s[g, c], sb))
        return tuple(new)

    lax.fori_loop(0, nc, scan, tuple(jnp.zeros((dh, dh), F32) for _ in range(heads)))

    for g in range(heads):
        ls = slice(g * dh, (g + 1) * dh)
        wq, attn, u = prepared[g]
        r = _bmm(wq, sall_s[g])
        v_new = u - r[:, :ck]
        o = (r[:, ck:] + _bmm(attn, v_new.astype(BF16))).reshape(seq, dh)
        gate = gate_ref[0, :, ls].astype(F32)
        o_ref[0, :, ls] = (_rms_rows(o, og_ref[...]) * _silu(gate)).astype(o_ref.dtype)


def delta_mix(proj, ba, conv_w, o_g, *, batch, seq, heads=2):
    nh = C_HEADS
    ngroup = nh // heads
    width = heads * C_HEAD_DIM
    nc = seq // C_CHUNK

    def slab(off):
        return pl.BlockSpec((1, seq, width), lambda b, h: (b, 0, off * ngroup + h))

    def conv_slab(off):
        return pl.BlockSpec((conv_w.shape[0], width), lambda b, h: (0, off * ngroup + h))

    return pl.pallas_call(
        functools.partial(_delta_kernel, heads=heads),
        grid=(batch, ngroup),
        in_specs=[slab(0), slab(1), slab(2), slab(3),
                  pl.BlockSpec((1, seq, LANES), lambda b, h: (b, 0, 0)),
                  conv_slab(0), conv_slab(1), conv_slab(2), _resident((1, LANES))],
        out_specs=pl.BlockSpec((1, seq, width), lambda b, h: (b, 0, h)),
        out_shape=jax.ShapeDtypeStruct((batch, seq, nh * C_HEAD_DIM), BF16),
        scratch_shapes=[pltpu.VMEM((heads, nc, C_HEAD_DIM, C_HEAD_DIM), BF16),
                        pltpu.VMEM((heads, nc, C_HEAD_DIM, C_HEAD_DIM), F32),
                        pltpu.VMEM((heads, nc, 1, LANES), F32),
                        pltpu.VMEM((heads, nc, C_HEAD_DIM, C_HEAD_DIM), BF16),
                        pltpu.VMEM((seq + SUBLANES, C_HEAD_DIM), F32)],
        compiler_params=_params("parallel", "parallel"),
        name="delta_mix",
    )(proj, proj, proj, proj, ba, conv_w, conv_w, conv_w, o_g.reshape(1, LANES))


def kernel(x, mem, norm_mix, norm_xattn, norm_mem, norm_ffn, ev_w_in, ev_conv, ev_q_norm, ev_k_norm, ev_w_out, od_w_in, od_conv, od_a_log, od_dt_bias, od_o_norm, od_w_out, xa_w_q, xa_w_kv, xa_q_norm, xa_k_norm, xa_w_o, ff_w_gu, ff_w_down, moe_router, moe_w_gu, moe_w_down):
    batch, seq, d = x.shape
    n = batch * seq
    h = x.reshape(n, d)
    depth = norm_mix.shape[0]
    for layer in range(depth):
        i = layer // 2
        g_mix = norm_mix[layer].reshape(1, d)
        g_ffn = norm_ffn[layer].reshape(1, d)
        kv = mem_kv(mem, norm_mem[layer].reshape(1, d), xa_w_kv[layer].astype(BF16), xa_k_norm[layer])
        xattn_args = (norm_xattn[layer].reshape(1, d), xa_w_q[layer].astype(BF16), xa_q_norm[layer], kv,
                      xa_w_o[layer].astype(BF16))
        if layer % 2 == 0:
            proj = norm_proj(h, g_mix, ev_w_in[i].astype(BF16))
            a, o = ev_mix(proj.reshape(batch, seq, -1), ev_conv[i], ev_q_norm[i], ev_k_norm[i],
                          batch=batch, seq=seq)
            w_out = ev_w_out[i].astype(BF16)
            h = layer_tail(h, [a.reshape(n, -1), o.reshape(n, -1)], [w_out[:A_WIDTH], w_out[A_WIDTH:]],
                           *xattn_args, (g_ffn, ff_w_gu[i].astype(BF16), ff_w_down[i].astype(BF16)), seq=seq)
        else:
            n_main = 4 * C_HEADS * C_HEAD_DIM
            w_gates = _split_weight(od_w_in[i][:, n_main:])
            proj, ba = norm_proj(h, g_mix, od_w_in[i][:, :n_main].astype(BF16), w_gates, od_a_log[i],
                                 od_dt_bias[i])
            o = delta_mix(proj.reshape(batch, seq, -1), ba.reshape(batch, seq, -1), od_conv[i], od_o_norm[i],
                          batch=batch, seq=seq)
            h = layer_tail(h, [o.reshape(n, -1)], [od_w_out[i].astype(BF16)], *xattn_args, seq=seq)
            h = moe(h, g_ffn, moe_router[i], moe_w_gu[i].astype(BF16), moe_w_down[i].astype(BF16))
    return h.reshape(batch, seq, d)
```

```python
import functools

import jax
import jax.numpy as jnp
from jax import lax
from jax.experimental import pallas as pl
from jax.experimental.pallas import tpu as pltpu

F32 = jnp.float32
BF16 = jnp.bfloat16

EPS = 1e-6
D_MODEL = 1024
A_WIDTH = 512
B_HEADS = 8
B_HEAD_DIM = 64
DILATED_PATTERN = ((128, 1), (512, 4), (2048, 16))
C_HEADS = 8
C_HEAD_DIM = 128
C_CHUNK = 64
X_HEADS = 4
X_HEAD_DIM = 64
N_EXPERTS = 8

LANES = 128
SUBLANES = 8
VMEM_LIMIT = 56 * 1024 * 1024
NEG = -1e30
LOG2_E = 1.4426950408889634


def _params(*sem):
    return pltpu.CompilerParams(dimension_semantics=sem, vmem_limit_bytes=VMEM_LIMIT)


def _resident(shape):
    nd = len(shape)
    return pl.BlockSpec(shape, lambda *_: (0,) * nd, pipeline_mode=pl.Buffered(1))


def _rms_rows(x, g):
    ms = jnp.mean(x * x, axis=-1, keepdims=True)
    return x * lax.rsqrt(ms + EPS) * g


def _silu(x):
    return x * jax.nn.sigmoid(x)


def _dot(a, b):
    return jnp.dot(a, b, preferred_element_type=F32)


def _dot_nt(a, b):
    return lax.dot_general(a, b, (((1,), (1,)), ((), ())), preferred_element_type=F32)


def _head_inv_rms(x, head_dim):
    width = x.shape[-1]
    lane = lax.broadcasted_iota(jnp.int32, (1, width), 1)
    x2 = x * x
    inv = jnp.zeros_like(x)
    for h in range(width // head_dim):
        sel = (lane >= h * head_dim) & (lane < (h + 1) * head_dim)
        ms = jnp.sum(jnp.where(sel, x2, 0.0), axis=-1, keepdims=True) * (1.0 / head_dim)
        inv = jnp.where(sel, lax.rsqrt(ms + EPS), inv)
    return inv


def _norm_proj_kernel(x_ref, g_ref, w_ref, o_ref, *, chunk):
    xn = _rms_rows(x_ref[...], g_ref[...]).astype(BF16)
    for c in range(w_ref.shape[1] // chunk):
        cs = slice(c * chunk, (c + 1) * chunk)
        o_ref[:, cs] = _dot(xn, w_ref[:, cs]).astype(o_ref.dtype)


def _norm_proj_gates_kernel(x_ref, g_ref, w_ref, wg_ref, alog_ref, dtb_ref, o_ref, og_ref, *, chunk):
    xf = _rms_rows(x_ref[...], g_ref[...])
    xn = xf.astype(BF16)
    for c in range(w_ref.shape[1] // chunk):
        cs = slice(c * chunk, (c + 1) * chunk)
        o_ref[:, cs] = _dot(xn, w_ref[:, cs]).astype(o_ref.dtype)
    ba = _dot_split(xf, wg_ref)
    lane = lax.broadcasted_iota(jnp.int32, (1, LANES), 1)
    z = ba + dtb_ref[...]
    softplus = jnp.maximum(z, 0.0) + jnp.log(1.0 + jnp.exp(-jnp.abs(z)))
    og_ref[...] = jnp.where(lane < C_HEADS, jax.nn.sigmoid(ba), -jnp.exp(alog_ref[...]) * softplus)


def norm_proj(x, g, w, w_gates=None, a_log=None, dt_bias=None, *, tm=512, chunk=512):
    n, d = x.shape
    nout = w.shape[1]
    in_specs = [pl.BlockSpec((tm, d), lambda i: (i, 0)), _resident((1, d)), _resident((d, nout))]
    if w_gates is None:
        return pl.pallas_call(
            functools.partial(_norm_proj_kernel, chunk=chunk),
            grid=(n // tm,),
            in_specs=in_specs,
            out_specs=pl.BlockSpec((tm, nout), lambda i: (i, 0)),
            out_shape=jax.ShapeDtypeStruct((n, nout), BF16),
            compiler_params=_params("parallel"),
            name="norm_proj",
        )(x, g, w)
    ng = w_gates.shape[1] // 2
    nh = a_log.shape[0]
    alog = jnp.pad(a_log.reshape(1, nh), ((0, 0), (nh, ng - 2 * nh)))
    dtb = jnp.pad(dt_bias.reshape(1, nh), ((0, 0), (nh, ng - 2 * nh)))
    return pl.pallas_call(
        functools.partial(_norm_proj_gates_kernel, chunk=chunk),
        grid=(n // tm,),
        in_specs=in_specs + [_resident(w_gates.shape), _resident((1, ng)), _resident((1, ng))],
        out_specs=[pl.BlockSpec((tm, nout), lambda i: (i, 0)), pl.BlockSpec((tm, ng), lambda i: (i, 0))],
        out_shape=[jax.ShapeDtypeStruct((n, nout), BF16), jax.ShapeDtypeStruct((n, ng), F32)],
        compiler_params=_params("parallel"),
        name="norm_proj_gates",
    )(x, g, w, w_gates, alog, dtb)


def _dilation_bias(seq, blk):
    dist = (seq - blk) + jnp.arange(blk)[:, None] - jnp.arange(seq)[None, :]
    mult = jnp.zeros(dist.shape, F32)
    for window, d in DILATED_PATTERN:
        mult = mult + ((dist >= 0) & (dist % d == 0) & (dist <= window)).astype(F32)
    return jnp.where(mult > 0, jnp.log2(jnp.maximum(mult, 1.0)), NEG)


def _ev_mix_kernel(gb_ref, gc_ref, xc_ref, q_ref, k_ref, v_ref, cw_ref, qg_ref, kg_ref, bias_ref,
                   a_ref, o_ref, kn_ref, qh_ref, s_ref, *, blk, kchunk):
    seq = q_ref.shape[1]
    y = gc_ref[0].astype(F32) * xc_ref[0].astype(F32)
    row = lax.broadcasted_iota(jnp.int32, (seq, 1), 0)
    kw = cw_ref.shape[0]
    conv = y * cw_ref[kw - 1:kw, :]
    for sh in range(1, kw):
        ys = jnp.where(row >= sh, pltpu.roll(y, sh, axis=0), 0.0)
        conv = conv + ys * cw_ref[kw - 1 - sh:kw - sh, :]
    a_ref[0] = (gb_ref[0].astype(F32) * conv).astype(a_ref.dtype)

    lane = lax.broadcasted_iota(jnp.int32, (1, LANES), 1)
    first = lane < B_HEAD_DIM
    kf = k_ref[0].astype(F32)
    kn_ref[...] = (kf * _head_inv_rms(kf, B_HEAD_DIM) * kg_ref[...]).astype(BF16)
    scale = B_HEAD_DIM ** -0.5 * LOG2_E
    qf = q_ref[0].astype(F32)
    qn = qf * _head_inv_rms(qf, B_HEAD_DIM) * (qg_ref[...] * scale)
    for h in range(2):
        qh = jnp.where(first if h == 0 else jnp.logical_not(first), qn, 0.0).astype(BF16)
        qh_ref[h] = qh

    def lane_groups(t):
        return [t[:, c:c + LANES] for c in range(0, t.shape[1], LANES)]

    for qi in range(seq // blk):
        r0 = qi * blk
        kv_len = r0 + blk
        chunks = [(lo, min(lo + kchunk, kv_len)) for lo in range(0, kv_len, kchunk)]
        outs = []
        for h in range(2):
            qh = qh_ref[h, r0:r0 + blk, :]
            s_buf = s_ref.at[qi % 2, h]
            mv = None
            for lo, hi in chunks:
                c0 = lo + seq - blk - r0
                s = _dot_nt(qh, kn_ref[lo:hi, :]) + bias_ref[:, c0:c0 + hi - lo]
                s_buf[:, lo:hi] = s
                for part in lane_groups(s):
                    mv = part if mv is None else jnp.maximum(mv, part)
            m = jnp.max(mv, axis=-1, keepdims=True)
            lv = jnp.zeros((blk, LANES), F32)
            acc = jnp.zeros((blk, LANES), F32)
            for lo, hi in chunks:
                p = jnp.exp2(s_buf[:, lo:hi] - m)
                for part in lane_groups(p):
                    lv = lv + part
                acc = acc + _dot(p.astype(BF16), v_ref[0, lo:hi, :])
            outs.append(acc / jnp.sum(lv, axis=-1, keepdims=True))
        o_ref[0, r0:r0 + blk, :] = jnp.where(first, outs[0], outs[1]).astype(o_ref.dtype)


def ev_mix(proj, conv_w, q_g, k_g, *, batch, seq, blk=256, kchunk=256):
    nslab = A_WIDTH // LANES
    bias = _dilation_bias(seq, blk)
    qg2 = jnp.tile(q_g.reshape(1, -1), (1, LANES // B_HEAD_DIM))
    kg2 = jnp.tile(k_g.reshape(1, -1), (1, LANES // B_HEAD_DIM))

    def slab(off):
        return pl.BlockSpec((1, seq, LANES), lambda b, s: (b, 0, off * nslab + s))

    out_spec = pl.BlockSpec((1, seq, LANES), lambda b, s: (b, 0, s))
    return pl.pallas_call(
        functools.partial(_ev_mix_kernel, blk=blk, kchunk=kchunk),
        grid=(batch, nslab),
        in_specs=[slab(0), slab(1), slab(2), slab(3), slab(4), slab(5),
                  pl.BlockSpec((conv_w.shape[0], LANES), lambda b, s: (0, s)),
                  _resident((1, LANES)), _resident((1, LANES)), _resident(bias.shape)],
        out_specs=[out_spec, out_spec],
        out_shape=[jax.ShapeDtypeStruct((batch, seq, A_WIDTH), BF16)] * 2,
        scratch_shapes=[pltpu.VMEM((seq, LANES), BF16), pltpu.VMEM((2, seq, LANES), BF16),
                        pltpu.VMEM((2, 2, blk, seq), F32)],
        compiler_params=_params("parallel", "parallel"),
        name="ev_mix",
    )(proj, proj, proj, proj, proj, proj, conv_w, qg2, kg2, bias)


def _mem_kv_kernel(m_ref, g_ref, w_ref, kg_ref, o_ref):
    width = X_HEADS * X_HEAD_DIM
    mn = _rms_rows(m_ref[0], g_ref[...]).astype(BF16)
    kv = _dot(mn, w_ref[...])
    k = kv[:, :width]
    o_ref[0, :, :width] = (k * _head_inv_rms(k, X_HEAD_DIM) * kg_ref[...]).astype(o_ref.dtype)
    o_ref[0, :, width:] = kv[:, width:].astype(o_ref.dtype)


def mem_kv(mem, g, w_kv, k_g):
    b, m, d = mem.shape
    width = w_kv.shape[1]
    kg = jnp.tile(k_g.reshape(1, -1), (1, X_HEADS))
    return pl.pallas_call(
        _mem_kv_kernel,
        grid=(b,),
        in_specs=[pl.BlockSpec((1, m, d), lambda i: (i, 0, 0)), _resident((1, d)), _resident(w_kv.shape),
                  _resident(kg.shape)],
        out_specs=pl.BlockSpec((1, m, width), lambda i: (i, 0, 0)),
        out_shape=jax.ShapeDtypeStruct((b, m, width), BF16),
        compiler_params=_params("parallel"),
        name="mem_kv",
    )(mem, g, w_kv, kg)


def _xattn_update(x, g_ref, wq_ref, qg_ref, kv_ref, wo_ref):
    width = X_HEADS * X_HEAD_DIM
    u = _rms_rows(x, g_ref[...]).astype(BF16)
    q = _dot(u, wq_ref[...])
    q = q * _head_inv_rms(q, X_HEAD_DIM) * (qg_ref[...] * X_HEAD_DIM ** -0.5)
    k = kv_ref[0, :, :width]
    v = kv_ref[0, :, width:]
    lane = lax.broadcasted_iota(jnp.int32, (1, width), 1)
    out = jnp.zeros(q.shape, F32)
    for h in range(X_HEADS):
        sel = (lane >= h * X_HEAD_DIM) & (lane < (h + 1) * X_HEAD_DIM)
        s = _dot_nt(jnp.where(sel, q, 0.0).astype(BF16), k)
        p = jnp.exp(s - jnp.max(s, axis=-1, keepdims=True))
        oh = _dot(p.astype(BF16), v) / jnp.sum(p, axis=-1, keepdims=True)
        out = jnp.where(sel, oh, out)
    return x + _dot(out.astype(BF16), wo_ref[...])


def _swiglu_act(u, wgu_ref, act_ref, d_ff, chunk):
    for lo in range(0, d_ff, chunk):
        hi = min(lo + chunk, d_ff)
        gate = _dot(u, wgu_ref[:, lo:hi])
        up = _dot(u, wgu_ref[:, d_ff + lo:d_ff + hi])
        act_ref[:, lo:hi] = (_silu(gate) * up).astype(act_ref.dtype)


def _tail_kernel(*refs, n_in, with_ffn, chunk):
    h_ref = refs[0]
    xs = refs[1:1 + n_in]
    ws = refs[1 + n_in:1 + 2 * n_in]
    xattn_refs = refs[1 + 2 * n_in:6 + 2 * n_in]
    x = h_ref[...]
    for x_ref, w_ref in zip(xs, ws):
        x = x + _dot(x_ref[...], w_ref[...])
    x = _xattn_update(x, *xattn_refs)
    if with_ffn:
        gf_ref, wgu_ref, wd_ref, o_ref, act_ref = refs[6 + 2 * n_in:]
        u = _rms_rows(x, gf_ref[...]).astype(BF16)
        _swiglu_act(u, wgu_ref, act_ref, wd_ref.shape[0], chunk)
        x = x + _dot(act_ref[...], wd_ref[...])
    else:
        o_ref = refs[-1]
    o_ref[...] = x


def layer_tail(h, xs, ws, g_x, w_q, q_g, kv, w_o, ffn_weights=None, *, seq, tm=512, chunk=256):
    n, d = h.shape
    qg = jnp.tile(q_g.reshape(1, -1), (1, X_HEADS))
    per_seq = seq // tm
    row = lambda i: (i, 0)
    in_specs = [pl.BlockSpec((tm, d), row)]
    in_specs += [pl.BlockSpec((tm, x.shape[1]), row) for x in xs]
    in_specs += [_resident(w.shape) for w in ws]
    in_specs += [_resident((1, d)), _resident(w_q.shape), _resident(qg.shape),
                 pl.BlockSpec((1,) + kv.shape[1:], lambda i: (i // per_seq, 0, 0)), _resident(w_o.shape)]
    args = [h, *xs, *ws, g_x, w_q, qg, kv, w_o]
    scratch = []
    if ffn_weights is not None:
        g_f, w_gu, w_down = ffn_weights
        in_specs += [_resident((1, d)), _resident(w_gu.shape), _resident(w_down.shape)]
        args += [g_f, w_gu, w_down]
        scratch = [pltpu.VMEM((tm, w_down.shape[0]), BF16)]
    return pl.pallas_call(
        functools.partial(_tail_kernel, n_in=len(xs), with_ffn=ffn_weights is not None, chunk=chunk),
        grid=(n // tm,),
        in_specs=in_specs,
        out_specs=pl.BlockSpec((tm, d), row),
        out_shape=jax.ShapeDtypeStruct((n, d), F32),
        scratch_shapes=scratch,
        compiler_params=_params("parallel"),
        name="layer_tail",
    )(*args)


def _split_bf16(x):
    hi = x.astype(BF16)
    return hi, (x - hi.astype(F32)).astype(BF16)


def _dot_split(x, w2_ref):
    half = w2_ref.shape[1] // 2
    xh, xl = _split_bf16(x)
    both = _dot(xh, w2_ref[...])
    return both[:, :half] + both[:, half:] + _dot(xl, w2_ref[:, :half])


def _split_weight(w):
    w = jnp.pad(w, ((0, 0), (0, LANES - w.shape[1])))
    hi = w.astype(BF16)
    return jnp.concatenate([hi, (w - hi.astype(F32)).astype(BF16)], axis=1)


def _moe_kernel(h_ref, g_ref, r2_ref, tri_ref, wgu_ref, wd_ref, o_ref,
                u_ref, wts_ref, rank_ref, rank_t_ref, act_ref, cnt_ref, *, chunk, sizes, sub_t):
    e = pl.program_id(1)
    n_sub = h_ref.shape[0] // sub_t
    d_ff = wd_ref.shape[1]
    lane = lax.broadcasted_iota(jnp.int32, (1, LANES), 1)

    @pl.when(e == 0)
    def _():
        x = h_ref[...]
        uf = _rms_rows(x, g_ref[...])
        u_ref[...] = uf.astype(BF16)
        logits = _dot_split(uf, r2_ref)
        logits = jnp.where(lane < N_EXPERTS, logits, NEG)
        m1 = jnp.max(logits, axis=-1, keepdims=True)
        i1 = jnp.min(jnp.where(logits == m1, lane, LANES), axis=-1, keepdims=True)
        rest = jnp.where(lane == i1, NEG, logits)
        m2 = jnp.max(rest, axis=-1, keepdims=True)
        i2 = jnp.min(jnp.where(rest == m2, lane, LANES), axis=-1, keepdims=True)
        e2 = jnp.exp(m2 - m1)
        w1 = 1.0 / (1.0 + e2)
        wts_ref[...] = jnp.where(lane == i1, w1, 0.0) + jnp.where(lane == i2, e2 * w1, 0.0)
        sel = jnp.where(lane == i1, 1.0, jnp.where(lane == i2, 1.0, 0.0))
        for s in range(n_sub):
            sel_s = sel[s * sub_t:(s + 1) * sub_t]
            rank = jnp.where(sel_s > 0.0, _dot(tri_ref[...], sel_s.astype(BF16)), -1.0)
            rank_ref[s * sub_t:(s + 1) * sub_t, :] = rank
            rank_t_ref[s] = jnp.transpose(rank)
            counts = jnp.sum(sel_s, axis=0, keepdims=True)
            for ee in range(N_EXPERTS):
                cnt_ref[s * N_EXPERTS + ee] = jnp.sum(jnp.where(lane == ee, counts, 0.0)).astype(jnp.int32)
        o_ref[...] = x

    def run_block(first, size, rs, rank_col, w_col, rank_row):
        base = first.astype(F32)
        sub = lax.broadcasted_iota(jnp.int32, (size, 1), 0).astype(F32)
        lan = lax.broadcasted_iota(jnp.int32, (1, size), 1).astype(F32)
        act = act_ref.at[pl.ds(0, size)]
        pick = jnp.where(rank_row == base + sub, 1.0, 0.0).astype(BF16)
        xg = _dot(pick, u_ref[rs, :]).astype(BF16)
        _swiglu_act(xg, wgu_ref.at[0], act, d_ff, chunk)
        y = _dot(act[...], wd_ref[0]).astype(BF16)
        place = jnp.where(rank_col == base + lan, 1.0, 0.0).astype(BF16)
        o_ref[rs, :] += w_col * _dot(place, y)

    big = sizes[-1]
    for s in range(n_sub):
        rs = slice(s * sub_t, (s + 1) * sub_t)
        rank_col = jnp.sum(jnp.where(lane == e, rank_ref[rs, :], 0.0), axis=-1, keepdims=True)
        w_col = jnp.sum(jnp.where(lane == e, wts_ref[rs, :], 0.0), axis=-1, keepdims=True)
        rank_row = rank_t_ref[s, pl.ds(e, 1), :]
        args = (rs, rank_col, w_col, rank_row)
        count = cnt_ref[s * N_EXPERTS + e]
        n_full = count // big
        rem = count - n_full * big

        def full_block(b, carry, args=args):
            run_block(b * big, big, *args)
            return carry

        lax.fori_loop(0, n_full + (rem > sizes[-2]).astype(jnp.int32), full_block, 0)
        below = 0
        for size in sizes[:-1]:
            @pl.when(jnp.logical_and(rem > below, rem <= size))
            def _(size=size, args=args, n_full=n_full):
                run_block(n_full * big, size, *args)
            below = size


def moe(h, g, router, w_gu, w_down, *, tm=1024, sub_t=512, chunk=512, sizes=(96, 128, 160, 192)):
    n, d = h.shape
    n_exp, d_ff, _ = w_down.shape
    n_sub = tm // sub_t
    tri = jnp.tril(jnp.ones((sub_t, sub_t), BF16), -1)
    r2 = _split_weight(router)
    return pl.pallas_call(
        functools.partial(_moe_kernel, chunk=chunk, sizes=sizes, sub_t=sub_t),
        grid=(n // tm, n_exp),
        in_specs=[pl.BlockSpec((tm, d), lambda i, e: (i, 0)), _resident((1, d)), _resident(r2.shape),
                  _resident(tri.shape),
                  pl.BlockSpec((1, d, 2 * d_ff), lambda i, e: (e, 0, 0)),
                  pl.BlockSpec((1, d_ff, d), lambda i, e: (e, 0, 0))],
        out_specs=pl.BlockSpec((tm, d), lambda i, e: (i, 0)),
        out_shape=jax.ShapeDtypeStruct((n, d), F32),
        scratch_shapes=[pltpu.VMEM((tm, d), BF16), pltpu.VMEM((tm, LANES), F32), pltpu.VMEM((tm, LANES), F32),
                        pltpu.VMEM((n_sub, LANES, sub_t), F32), pltpu.VMEM((sizes[-1], d_ff), BF16),
                        pltpu.SMEM((n_sub * N_EXPERTS,), jnp.int32)],
        compiler_params=_params("parallel", "arbitrary"),
        name="moe",
    )(h, g, r2, tri, w_gu, w_down)


def _pair_mask(ii, jj, s):
    bi = ii // s
    return (bi % 2 == 1) & (jj // s == bi - 1)


def _bmm(a, b):
    return lax.dot_general(a, b, (((2,), (1,)), ((0,), (0,))), preferred_element_type=F32)


def _bmm_nt(a, b):
    return lax.dot_general(a, b, (((2,), (2,)), ((0,), (0,))), preferred_element_type=F32)


def _delta_kernel(q_ref, k_ref, v_ref, gate_ref, ba_ref, cq_ref, ck_ref, cv_ref, og_ref,
                  o_ref, p_s, n_s, egl_s, sall_s, xpad_s, *, heads):
    seq = q_ref.shape[1]
    ck = C_CHUNK
    nc = seq // ck
    dh = C_HEAD_DIM
    group = pl.program_id(1)
    row = lax.broadcasted_iota(jnp.int32, (seq, 1), 0)
    rc = row % ck
    lane = lax.broadcasted_iota(jnp.int32, (1, LANES), 1)
    ii = lax.broadcasted_iota(jnp.int32, (1, ck, ck), 1)
    jj = lax.broadcasted_iota(jnp.int32, (1, ck, ck), 2)
    eye = (ii == jj).astype(F32)

    beta_all = ba_ref[0]
    g_all = beta_all

    pad = xpad_s.shape[0] - seq
    xpad_s[0:pad, :] = jnp.zeros((pad, dh), F32)

    def conv_silu(x, w_ref, ls):
        kw = w_ref.shape[0]
        xpad_s[pad:, :] = x
        y = x * w_ref[kw - 1:kw, ls]
        for sh in range(1, kw):
            y = y + xpad_s[pad - sh:pad - sh + seq, :] * w_ref[kw - 1 - sh:kw - sh, ls]
        return _silu(y)

    def l2n(x):
        return x * lax.rsqrt(jnp.sum(x * x, axis=-1, keepdims=True) + EPS)

    prepared = []
    for g in range(heads):
        ls = slice(g * dh, (g + 1) * dh)
        head = group * heads + g
        q = l2n(conv_silu(q_ref[0, :, ls].astype(F32), cq_ref, ls)) * (dh ** -0.5)
        k = l2n(conv_silu(k_ref[0, :, ls].astype(F32), ck_ref, ls))
        v = conv_silu(v_ref[0, :, ls].astype(F32), cv_ref, ls)
        beta = jnp.broadcast_to(jnp.sum(jnp.where(lane == head, beta_all, 0.0), axis=-1, keepdims=True),
                                (seq, LANES))
        gcs = jnp.broadcast_to(jnp.sum(jnp.where(lane == head + C_HEADS, g_all, 0.0), axis=-1, keepdims=True),
                               (seq, LANES))
        sh = 1
        while sh < ck:
            gcs = gcs + jnp.where(rc >= sh, pltpu.roll(gcs, sh, axis=0), 0.0)
            sh *= 2
        q3, k3, v3 = (t.reshape(nc, ck, dh) for t in (q, k, v))
        beta3 = beta.reshape(nc, ck, LANES)
        gcs3 = gcs.reshape(nc, ck, LANES)
        egc3 = jnp.exp(gcs3)
        glast = gcs3[:, ck - 1:ck, :]
        gc_row = jnp.swapaxes(gcs3, 1, 2)[:, :ck, :]
        decay = jnp.exp(jnp.where(ii >= jj, gcs3[:, :, :ck] - gc_row, NEG))
        kb = k3.astype(BF16)
        qk = _bmm_nt(jnp.concatenate([q3.astype(BF16), kb], axis=1), kb)
        attn = (qk[:, :ck] * decay).astype(BF16)
        a_mat = jnp.where(ii > jj, beta3[:, :, :ck] * qk[:, ck:] * decay, 0.0)
        t = eye - jnp.where(_pair_mask(ii, jj, 1), a_mat, 0.0)
        s = 2
        while s < ck:
            tb = t.astype(BF16)
            x = _bmm(tb, jnp.where(_pair_mask(ii, jj, s), a_mat, 0.0).astype(BF16))
            t = t - _bmm(x.astype(BF16), tb)
            s *= 2
        rhs = jnp.concatenate([v3 * beta3, k3 * (beta3 * egc3)], axis=2).astype(BF16)
        uw = _bmm(t.astype(BF16), rhs)
        kd_t = jnp.swapaxes(k3 * jnp.exp(glast - gcs3), 1, 2).astype(BF16)
        pn = _bmm(kd_t, uw.astype(BF16))
        n_s[g] = pn[:, :, :dh]
        p_s[g] = pn[:, :, dh:].astype(BF16)
        egl_s[g] = jnp.exp(glast)
        wq = jnp.concatenate([uw[:, :, dh:], q3 * egc3], axis=1).astype(BF16)
        prepared.append((wq, attn, uw[:, :, :dh]))

    def scan(c, states):
        new = []
        for g in range(heads):
            sb = states[g].astype(BF16)
            sall_s[g, c] = sb
            new.append(states[g] * egl_s[g, c] + n_s[g, c] - _dot(p_s[g, c], sb))
        return tuple(new)

    lax.fori_loop(0, nc, scan, tuple(jnp.zeros((dh, dh), F32) for _ in range(heads)))

    for g in range(heads):
        ls = slice(g * dh, (g + 1) * dh)
        wq, attn, u = prepared[g]
        r = _bmm(wq, sall_s[g])
        v_new = u - r[:, :ck]
        o = (r[:, ck:] + _bmm(attn, v_new.astype(BF16))).reshape(seq, dh)
        gate = gate_ref[0, :, ls].astype(F32)
        o_ref[0, :, ls] = (_rms_rows(o, og_ref[...]) * _silu(gate)).astype(o_ref.dtype)


def delta_mix(proj, ba, conv_w, o_g, *, batch, seq, heads=2):
    nh = C_HEADS
    ngroup = nh // heads
    width = heads * C_HEAD_DIM
    nc = seq // C_CHUNK

    def slab(off):
        return pl.BlockSpec((1, seq, width), lambda b, h: (b, 0, off * ngroup + h))

    def conv_slab(off):
        return pl.BlockSpec((conv_w.shape[0], width), lambda b, h: (0, off * ngroup + h))

    return pl.pallas_call(
        functools.partial(_delta_kernel, heads=heads),
        grid=(batch, ngroup),
        in_specs=[slab(0), slab(1), slab(2), slab(3),
                  pl.BlockSpec((1, seq, LANES), lambda b, h: (b, 0, 0)),
                  conv_slab(0), conv_slab(1), conv_slab(2), _resident((1, LANES))],
        out_specs=pl.BlockSpec((1, seq, width), lambda b, h: (b, 0, h)),
        out_shape=jax.ShapeDtypeStruct((batch, seq, nh * C_HEAD_DIM), BF16),
        scratch_shapes=[pltpu.VMEM((heads, nc, C_HEAD_DIM, C_HEAD_DIM), BF16),
                        pltpu.VMEM((heads, nc, C_HEAD_DIM, C_HEAD_DIM), F32),
                        pltpu.VMEM((heads, nc, 1, LANES), F32),
                        pltpu.VMEM((heads, nc, C_HEAD_DIM, C_HEAD_DIM), BF16),
                        pltpu.VMEM((seq + SUBLANES, C_HEAD_DIM), F32)],
        compiler_params=_params("parallel", "parallel"),
        name="delta_mix",
    )(proj, proj, proj, proj, ba, conv_w, conv_w, conv_w, o_g.reshape(1, LANES))


def kernel(x, mem, norm_mix, norm_xattn, norm_mem, norm_ffn, ev_w_in, ev_conv, ev_q_norm, ev_k_norm, ev_w_out, od_w_in, od_conv, od_a_log, od_dt_bias, od_o_norm, od_w_out, xa_w_q, xa_w_kv, xa_q_norm, xa_k_norm, xa_w_o, ff_w_gu, ff_w_down, moe_router, moe_w_gu, moe_w_down):
    batch, seq, d = x.shape
    n = batch * seq
    h = x.reshape(n, d)
    depth = norm_mix.shape[0]
    for layer in range(depth):
        i = layer // 2
        g_mix = norm_mix[layer].reshape(1, d)
        g_ffn = norm_ffn[layer].reshape(1, d)
        kv = mem_kv(mem, norm_mem[layer].reshape(1, d), xa_w_kv[layer].astype(BF16), xa_k_norm[layer])
        xattn_args = (norm_xattn[layer].reshape(1, d), xa_w_q[layer].astype(BF16), xa_q_norm[layer], kv,
                      xa_w_o[layer].astype(BF16))
        if layer % 2 == 0:
            proj = norm_proj(h, g_mix, ev_w_in[i].astype(BF16))
            a, o = ev_mix(proj.reshape(batch, seq, -1), ev_conv[i], ev_q_norm[i], ev_k_norm[i],
                          batch=batch, seq=seq)
            w_out = ev_w_out[i].astype(BF16)
            h = layer_tail(h, [a.reshape(n, -1), o.reshape(n, -1)], [w_out[:A_WIDTH], w_out[A_WIDTH:]],
                           *xattn_args, (g_ffn, ff_w_gu[i].astype(BF16), ff_w_down[i].astype(BF16)), seq=seq)
        else:
            n_main = 4 * C_HEADS * C_HEAD_DIM
            w_gates = _split_weight(od_w_in[i][:, n_main:])
            proj, ba = norm_proj(h, g_mix, od_w_in[i][:, :n_main].astype(BF16), w_gates, od_a_log[i],
                                 od_dt_bias[i])
            o = delta_mix(proj.reshape(batch, seq, -1), ba.reshape(batch, seq, -1), od_conv[i], od_o_norm[i],
                          batch=batch, seq=seq)
            h = layer_tail(h, [o.reshape(n, -1)], [od_w_out[i].astype(BF16)], *xattn_args, seq=seq)
            h = moe(h, g_ffn, moe_router[i], moe_w_gu[i].astype(BF16), moe_w_down[i].astype(BF16))
    return h.reshape(batch, seq, d)
```

```python
import functools

import jax
import jax.numpy as jnp
from jax import lax
from jax.experimental import pallas as pl
from jax.experimental.pallas import tpu as pltpu

F32 = jnp.float32
BF16 = jnp.bfloat16

EPS = 1e-6
D_MODEL = 1024
A_WIDTH = 512
B_HEADS = 8
B_HEAD_DIM = 64
DILATED_PATTERN = ((128, 1), (512, 4), (2048, 16))
C_HEADS = 8
C_HEAD_DIM = 128
C_CHUNK = 64
X_HEADS = 4
X_HEAD_DIM = 64
N_EXPERTS = 8

LANES = 128
SUBLANES = 8
VMEM_LIMIT = 56 * 1024 * 1024
NEG = -1e30
LOG2_E = 1.4426950408889634


def _params(*sem):
    return pltpu.CompilerParams(dimension_semantics=sem, vmem_limit_bytes=VMEM_LIMIT)


def _resident(shape):
    nd = len(shape)
    return pl.BlockSpec(shape, lambda *_: (0,) * nd, pipeline_mode=pl.Buffered(1))


def _rms_rows(x, g):
    ms = jnp.mean(x * x, axis=-1, keepdims=True)
    return x * lax.rsqrt(ms + EPS) * g


def _silu(x):
    return x * jax.nn.sigmoid(x)


def _dot(a, b):
    return jnp.dot(a, b, preferred_element_type=F32)


def _dot_nt(a, b):
    return lax.dot_general(a, b, (((1,), (1,)), ((), ())), preferred_element_type=F32)


def _dot_tn(a, b):
    return lax.dot_general(a, b, (((0,), (0,)), ((), ())), preferred_element_type=F32)


def _head_inv_rms(x, head_dim):
    width = x.shape[-1]
    lane = lax.broadcasted_iota(jnp.int32, (1, width), 1)
    x2 = x * x
    inv = jnp.zeros_like(x)
    for h in range(width // head_dim):
        sel = (lane >= h * head_dim) & (lane < (h + 1) * head_dim)
        ms = jnp.sum(jnp.where(sel, x2, 0.0), axis=-1, keepdims=True) * (1.0 / head_dim)
        inv = jnp.where(sel, lax.rsqrt(ms + EPS), inv)
    return inv


def _norm_proj_kernel(x_ref, g_ref, w_ref, o_ref, *, chunk):
    xn = _rms_rows(x_ref[...], g_ref[...]).astype(BF16)
    for c in range(w_ref.shape[1] // chunk):
        cs = slice(c * chunk, (c + 1) * chunk)
        o_ref[:, cs] = _dot(xn, w_ref[:, cs]).astype(o_ref.dtype)


def _norm_proj_gates_kernel(x_ref, g_ref, w_ref, wg_ref, alog_ref, dtb_ref, o_ref, og_ref, *, chunk):
    xf = _rms_rows(x_ref[...], g_ref[...])
    xn = xf.astype(BF16)
    for c in range(w_ref.shape[1] // chunk):
        cs = slice(c * chunk, (c + 1) * chunk)
        o_ref[:, cs] = _dot(xn, w_ref[:, cs]).astype(o_ref.dtype)
    ba = _dot_split(xf, wg_ref)
    lane = lax.broadcasted_iota(jnp.int32, (1, LANES), 1)
    z = ba + dtb_ref[...]
    softplus = jnp.maximum(z, 0.0) + jnp.log(1.0 + jnp.exp(-jnp.abs(z)))
    og_ref[...] = jnp.where(lane < C_HEADS, jax.nn.sigmoid(ba), -jnp.exp(alog_ref[...]) * softplus)


def norm_proj(x, g, w, w_gates=None, a_log=None, dt_bias=None, *, tm=512, chunk=512):
    n, d = x.shape
    nout = w.shape[1]
    in_specs = [pl.BlockSpec((tm, d), lambda i: (i, 0)), _resident((1, d)), _resident((d, nout))]
    if w_gates is None:
        return pl.pallas_call(
            functools.partial(_norm_proj_kernel, chunk=chunk),
            grid=(n // tm,),
            in_specs=in_specs,
            out_specs=pl.BlockSpec((tm, nout), lambda i: (i, 0)),
            out_shape=jax.ShapeDtypeStruct((n, nout), BF16),
            compiler_params=_params("parallel"),
            name="norm_proj",
        )(x, g, w)
    ng = w_gates.shape[1] // 2
    nh = a_log.shape[0]
    alog = jnp.pad(a_log.reshape(1, nh), ((0, 0), (nh, ng - 2 * nh)))
    dtb = jnp.pad(dt_bias.reshape(1, nh), ((0, 0), (nh, ng - 2 * nh)))
    return pl.pallas_call(
        functools.partial(_norm_proj_gates_kernel, chunk=chunk),
        grid=(n // tm,),
        in_specs=in_specs + [_resident(w_gates.shape), _resident((1, ng)), _resident((1, ng))],
        out_specs=[pl.BlockSpec((tm, nout), lambda i: (i, 0)), pl.BlockSpec((tm, ng), lambda i: (i, 0))],
        out_shape=[jax.ShapeDtypeStruct((n, nout), BF16), jax.ShapeDtypeStruct((n, ng), F32)],
        compiler_params=_params("parallel"),
        name="norm_proj_gates",
    )(x, g, w, w_gates, alog, dtb)


def _dilation_bias(seq, blk):
    dist = (seq - blk) + jnp.arange(blk)[:, None] - jnp.arange(seq)[None, :]
    mult = jnp.zeros(dist.shape, F32)
    for window, d in DILATED_PATTERN:
        mult = mult + ((dist >= 0) & (dist % d == 0) & (dist <= window)).astype(F32)
    return jnp.where(mult > 0, jnp.log2(jnp.maximum(mult, 1.0)), NEG)


def _ev_mix_kernel(gb_ref, gc_ref, xc_ref, q_ref, k_ref, v_ref, cw_ref, qg_ref, kg_ref, bias_ref,
                   a_ref, o_ref, kn_ref, qh_ref, s_ref, *, blk, kchunk):
    seq = q_ref.shape[1]
    y = gc_ref[0].astype(F32) * xc_ref[0].astype(F32)
    row = lax.broadcasted_iota(jnp.int32, (seq, 1), 0)
    kw = cw_ref.shape[0]
    conv = y * cw_ref[kw - 1:kw, :]
    for sh in range(1, kw):
        ys = jnp.where(row >= sh, pltpu.roll(y, sh, axis=0), 0.0)
        conv = conv + ys * cw_ref[kw - 1 - sh:kw - sh, :]
    a_ref[0] = (gb_ref[0].astype(F32) * conv).astype(a_ref.dtype)

    lane = lax.broadcasted_iota(jnp.int32, (1, LANES), 1)
    first = lane < B_HEAD_DIM
    kf = k_ref[0].astype(F32)
    kn_ref[...] = (kf * _head_inv_rms(kf, B_HEAD_DIM) * kg_ref[...]).astype(BF16)
    scale = B_HEAD_DIM ** -0.5 * LOG2_E
    qf = q_ref[0].astype(F32)
    qn = qf * _head_inv_rms(qf, B_HEAD_DIM) * (qg_ref[...] * scale)
    for h in range(2):
        qh = jnp.where(first if h == 0 else jnp.logical_not(first), qn, 0.0).astype(BF16)
        qh_ref[h] = qh

    def lane_groups(t):
        return [t[:, c:c + LANES] for c in range(0, t.shape[1], LANES)]

    for qi in range(seq // blk):
        r0 = qi * blk
        kv_len = r0 + blk
        chunks = [(lo, min(lo + kchunk, kv_len)) for lo in range(0, kv_len, kchunk)]
        outs = []
        for h in range(2):
            qh = qh_ref[h, r0:r0 + blk, :]
            s_buf = s_ref.at[qi % 2, h]
            mv = None
            for lo, hi in chunks:
                c0 = lo + seq - blk - r0
                s = _dot_nt(qh, kn_ref[lo:hi, :]) + bias_ref[:, c0:c0 + hi - lo]
                s_buf[:, lo:hi] = s
                for part in lane_groups(s):
                    mv = part if mv is None else jnp.maximum(mv, part)
            m = jnp.max(mv, axis=-1, keepdims=True)
            lv = jnp.zeros((blk, LANES), F32)
            acc = jnp.zeros((blk, LANES), F32)
            for lo, hi in chunks:
                p = jnp.exp2(s_buf[:, lo:hi] - m)
                for part in lane_groups(p):
                    lv = lv + part
                acc = acc + _dot(p.astype(BF16), v_ref[0, lo:hi, :])
            outs.append(acc / jnp.sum(lv, axis=-1, keepdims=True))
        o_ref[0, r0:r0 + blk, :] = jnp.where(first, outs[0], outs[1]).astype(o_ref.dtype)


def ev_mix(proj, conv_w, q_g, k_g, *, batch, seq, blk=256, kchunk=256):
    nslab = A_WIDTH // LANES
    bias = _dilation_bias(seq, blk)
    qg2 = jnp.tile(q_g.reshape(1, -1), (1, LANES // B_HEAD_DIM))
    kg2 = jnp.tile(k_g.reshape(1, -1), (1, LANES // B_HEAD_DIM))

    def slab(off):
        return pl.BlockSpec((1, seq, LANES), lambda b, s: (b, 0, off * nslab + s))

    out_spec = pl.BlockSpec((1, seq, LANES), lambda b, s: (b, 0, s))
    return pl.pallas_call(
        functools.partial(_ev_mix_kernel, blk=blk, kchunk=kchunk),
        grid=(batch, nslab),
        in_specs=[slab(0), slab(1), slab(2), slab(3), slab(4), slab(5),
                  pl.BlockSpec((conv_w.shape[0], LANES), lambda b, s: (0, s)),
                  _resident((1, LANES)), _resident((1, LANES)), _resident(bias.shape)],
        out_specs=[out_spec, out_spec],
        out_shape=[jax.ShapeDtypeStruct((batch, seq, A_WIDTH), BF16)] * 2,
        scratch_shapes=[pltpu.VMEM((seq, LANES), BF16), pltpu.VMEM((2, seq, LANES), BF16),
                        pltpu.VMEM((2, 2, blk, seq), F32)],
        compiler_params=_params("parallel", "parallel"),
        name="ev_mix",
    )(proj, proj, proj, proj, proj, proj, conv_w, qg2, kg2, bias)


def _mem_kv_kernel(m_ref, g_ref, w_ref, kg_ref, o_ref):
    width = X_HEADS * X_HEAD_DIM
    mn = _rms_rows(m_ref[0], g_ref[...]).astype(BF16)
    kv = _dot(mn, w_ref[...])
    k = kv[:, :width]
    o_ref[0, :, :width] = (k * _head_inv_rms(k, X_HEAD_DIM) * kg_ref[...]).astype(o_ref.dtype)
    o_ref[0, :, width:] = kv[:, width:].astype(o_ref.dtype)


def mem_kv(mem, g, w_kv, k_g):
    b, m, d = mem.shape
    width = w_kv.shape[1]
    kg = jnp.tile(k_g.reshape(1, -1), (1, X_HEADS))
    return pl.pallas_call(
        _mem_kv_kernel,
        grid=(b,),
        in_specs=[pl.BlockSpec((1, m, d), lambda i: (i, 0, 0)), _resident((1, d)), _resident(w_kv.shape),
                  _resident(kg.shape)],
        out_specs=pl.BlockSpec((1, m, width), lambda i: (i, 0, 0)),
        out_shape=jax.ShapeDtypeStruct((b, m, width), BF16),
        compiler_params=_params("parallel"),
        name="mem_kv",
    )(mem, g, w_kv, kg)


def _xattn_update(x, g_ref, wq_ref, qg_ref, kv_ref, wo_ref):
    width = X_HEADS * X_HEAD_DIM
    u = _rms_rows(x, g_ref[...]).astype(BF16)
    q = _dot(u, wq_ref[...])
    q = q * _head_inv_rms(q, X_HEAD_DIM) * (qg_ref[...] * X_HEAD_DIM ** -0.5)
    k = kv_ref[0, :, :width]
    v = kv_ref[0, :, width:]
    lane = lax.broadcasted_iota(jnp.int32, (1, width), 1)
    out = jnp.zeros(q.shape, F32)
    for h in range(X_HEADS):
        sel = (lane >= h * X_HEAD_DIM) & (lane < (h + 1) * X_HEAD_DIM)
        s = _dot_nt(jnp.where(sel, q, 0.0).astype(BF16), k)
        p = jnp.exp(s - jnp.max(s, axis=-1, keepdims=True))
        oh = _dot(p.astype(BF16), v) / jnp.sum(p, axis=-1, keepdims=True)
        out = jnp.where(sel, oh, out)
    return x + _dot(out.astype(BF16), wo_ref[...])


def _swiglu_act(u, wgu_ref, act_ref, d_ff, chunk):
    for lo in range(0, d_ff, chunk):
        hi = min(lo + chunk, d_ff)
        gate = _dot(u, wgu_ref[:, lo:hi])
        up = _dot(u, wgu_ref[:, d_ff + lo:d_ff + hi])
        act_ref[:, lo:hi] = (_silu(gate) * up).astype(act_ref.dtype)


def _tail_kernel(*refs, n_in, with_ffn, chunk):
    h_ref = refs[0]
    xs = refs[1:1 + n_in]
    ws = refs[1 + n_in:1 + 2 * n_in]
    xattn_refs = refs[1 + 2 * n_in:6 + 2 * n_in]
    x = h_ref[...]
    for x_ref, w_ref in zip(xs, ws):
        x = x + _dot(x_ref[...], w_ref[...])
    x = _xattn_update(x, *xattn_refs)
    if with_ffn:
        gf_ref, wgu_ref, wd_ref, o_ref, act_ref = refs[6 + 2 * n_in:]
        u = _rms_rows(x, gf_ref[...]).astype(BF16)
        _swiglu_act(u, wgu_ref, act_ref, wd_ref.shape[0], chunk)
        x = x + _dot(act_ref[...], wd_ref[...])
    else:
        o_ref = refs[-1]
    o_ref[...] = x


def layer_tail(h, xs, ws, g_x, w_q, q_g, kv, w_o, ffn_weights=None, *, seq, tm=512, chunk=256):
    n, d = h.shape
    qg = jnp.tile(q_g.reshape(1, -1), (1, X_HEADS))
    per_seq = seq // tm
    row = lambda i: (i, 0)
    in_specs = [pl.BlockSpec((tm, d), row)]
    in_specs += [pl.BlockSpec((tm, x.shape[1]), row) for x in xs]
    in_specs += [_resident(w.shape) for w in ws]
    in_specs += [_resident((1, d)), _resident(w_q.shape), _resident(qg.shape),
                 pl.BlockSpec((1,) + kv.shape[1:], lambda i: (i // per_seq, 0, 0)), _resident(w_o.shape)]
    args = [h, *xs, *ws, g_x, w_q, qg, kv, w_o]
    scratch = []
    if ffn_weights is not None:
        g_f, w_gu, w_down = ffn_weights
        in_specs += [_resident((1, d)), _resident(w_gu.shape), _resident(w_down.shape)]
        args += [g_f, w_gu, w_down]
        scratch = [pltpu.VMEM((tm, w_down.shape[0]), BF16)]
    return pl.pallas_call(
        functools.partial(_tail_kernel, n_in=len(xs), with_ffn=ffn_weights is not None, chunk=chunk),
        grid=(n // tm,),
        in_specs=in_specs,
        out_specs=pl.BlockSpec((tm, d), row),
        out_shape=jax.ShapeDtypeStruct((n, d), F32),
        scratch_shapes=scratch,
        compiler_params=_params("parallel"),
        name="layer_tail",
    )(*args)


def _split_bf16(x):
    hi = x.astype(BF16)
    return hi, (x - hi.astype(F32)).astype(BF16)


def _dot_split(x, w2_ref):
    half = w2_ref.shape[1] // 2
    xh, xl = _split_bf16(x)
    both = _dot(xh, w2_ref[...])
    return both[:, :half] + both[:, half:] + _dot(xl, w2_ref[:, :half])


def _split_weight(w):
    w = jnp.pad(w, ((0, 0), (0, LANES - w.shape[1])))
    hi = w.astype(BF16)
    return jnp.concatenate([hi, (w - hi.astype(F32)).astype(BF16)], axis=1)


def _moe_kernel(h_ref, g_ref, r2_ref, tri_ref, wgu_ref, wd_ref, o_ref,
                u_ref, rank_t_ref, wts_t_ref, act_ref, cnt_ref, *, chunk, sizes, sub_t):
    e = pl.program_id(1)
    n_sub = h_ref.shape[0] // sub_t
    d_ff = wd_ref.shape[1]
    lane = lax.broadcasted_iota(jnp.int32, (1, LANES), 1)

    @pl.when(e == 0)
    def _():
        x = h_ref[...]
        uf = _rms_rows(x, g_ref[...])
        u_ref[...] = uf.astype(BF16)
        logits = _dot_split(uf, r2_ref)
        logits = jnp.where(lane < N_EXPERTS, logits, NEG)
        m1 = jnp.max(logits, axis=-1, keepdims=True)
        i1 = jnp.min(jnp.where(logits == m1, lane, LANES), axis=-1, keepdims=True)
        rest = jnp.where(lane == i1, NEG, logits)
        m2 = jnp.max(rest, axis=-1, keepdims=True)
        i2 = jnp.min(jnp.where(rest == m2, lane, LANES), axis=-1, keepdims=True)
        e2 = jnp.exp(m2 - m1)
        w1 = 1.0 / (1.0 + e2)
        wts = jnp.where(lane == i1, w1, 0.0) + jnp.where(lane == i2, e2 * w1, 0.0)
        sel = jnp.where(lane == i1, 1.0, jnp.where(lane == i2, 1.0, 0.0))
        for s in range(n_sub):
            rs = slice(s * sub_t, (s + 1) * sub_t)
            sel_s = sel[rs]
            rank = jnp.where(sel_s > 0.0, _dot(tri_ref[...], sel_s.astype(BF16)), -1.0)
            rank_t_ref[s] = jnp.transpose(rank)
            wts_t_ref[s] = jnp.transpose(wts[rs])
            counts = jnp.sum(sel_s, axis=0, keepdims=True)
            for ee in range(N_EXPERTS):
                cnt_ref[s * N_EXPERTS + ee] = jnp.sum(jnp.where(lane == ee, counts, 0.0)).astype(jnp.int32)
        o_ref[...] = x

    def run_block(first, size, rs, rank_row, w_row):
        sub = lax.broadcasted_iota(jnp.int32, (size, 1), 0).astype(F32)
        act = act_ref.at[pl.ds(0, size)]
        hit = rank_row == first.astype(F32) + sub
        pick = jnp.where(hit, 1.0, 0.0).astype(BF16)
        w_rows = jnp.sum(jnp.where(hit, w_row, 0.0), axis=-1, keepdims=True)
        xg = _dot(pick, u_ref[rs, :]).astype(BF16)
        _swiglu_act(xg, wgu_ref.at[0], act, d_ff, chunk)
        y = (_dot(act[...], wd_ref[0]) * w_rows).astype(BF16)
        o_ref[rs, :] += _dot_tn(pick, y)

    big = sizes[-1]
    for s in range(n_sub):
        rs = slice(s * sub_t, (s + 1) * sub_t)
        args = (rs, rank_t_ref[s, pl.ds(e, 1), :], wts_t_ref[s, pl.ds(e, 1), :])
        count = cnt_ref[s * N_EXPERTS + e]
        n_full = count // big
        rem = count - n_full * big

        def full_block(b, carry, args=args):
            run_block(b * big, big, *args)
            return carry

        lax.fori_loop(0, n_full + (rem > sizes[-2]).astype(jnp.int32), full_block, 0)
        below = 0
        for size in sizes[:-1]:
            @pl.when(jnp.logical_and(rem > below, rem <= size))
            def _(size=size, args=args, n_full=n_full):
                run_block(n_full * big, size, *args)
            below = size


def moe(h, g, router, w_gu, w_down, *, tm=1024, sub_t=512, chunk=512, sizes=(128, 160, 192)):
    n, d = h.shape
    n_exp, d_ff, _ = w_down.shape
    n_sub = tm // sub_t
    tri = jnp.tril(jnp.ones((sub_t, sub_t), BF16), -1)
    r2 = _split_weight(router)
    return pl.pallas_call(
        functools.partial(_moe_kernel, chunk=chunk, sizes=sizes, sub_t=sub_t),
        grid=(n // tm, n_exp),
        in_specs=[pl.BlockSpec((tm, d), lambda i, e: (i, 0)), _resident((1, d)), _resident(r2.shape),
                  _resident(tri.shape),
                  pl.BlockSpec((1, d, 2 * d_ff), lambda i, e: (e, 0, 0)),
                  pl.BlockSpec((1, d_ff, d), lambda i, e: (e, 0, 0))],
        out_specs=pl.BlockSpec((tm, d), lambda i, e: (i, 0)),
        out_shape=jax.ShapeDtypeStruct((n, d), F32),
        scratch_shapes=[pltpu.VMEM((tm, d), BF16), pltpu.VMEM((n_sub, LANES, sub_t), F32),
                        pltpu.VMEM((n_sub, LANES, sub_t), F32), pltpu.VMEM((sizes[-1], d_ff), BF16),
                        pltpu.SMEM((n_sub * N_EXPERTS,), jnp.int32)],
        compiler_params=_params("parallel", "arbitrary"),
        name="moe",
    )(h, g, r2, tri, w_gu, w_down)


def _pair_mask(ii, jj, s):
    bi = ii // s
    return (bi % 2 == 1) & (jj // s == bi - 1)


def _bmm(a, b):
    return lax.dot_general(a, b, (((2,), (1,)), ((0,), (0,))), preferred_element_type=F32)


def _bmm_nt(a, b):
    return lax.dot_general(a, b, (((2,), (2,)), ((0,), (0,))), preferred_element_type=F32)


def _delta_kernel(q_ref, k_ref, v_ref, gate_ref, ba_ref, cq_ref, ck_ref, cv_ref, og_ref,
                  o_ref, p_s, n_s, egl_s, sall_s, xpad_s, *, heads):
    seq = q_ref.shape[1]
    ck = C_CHUNK
    nc = seq // ck
    dh = C_HEAD_DIM
    group = pl.program_id(1)
    row = lax.broadcasted_iota(jnp.int32, (seq, 1), 0)
    rc = row % ck
    lane = lax.broadcasted_iota(jnp.int32, (1, LANES), 1)
    ii = lax.broadcasted_iota(jnp.int32, (1, ck, ck), 1)
    jj = lax.broadcasted_iota(jnp.int32, (1, ck, ck), 2)
    eye = (ii == jj).astype(F32)

    beta_all = ba_ref[0]
    g_all = beta_all

    pad = xpad_s.shape[0] - seq
    xpad_s[0:pad, :] = jnp.zeros((pad, dh), F32)

    def conv_silu(x, w_ref, ls):
        kw = w_ref.shape[0]
        xpad_s[pad:, :] = x
        y = x * w_ref[kw - 1:kw, ls]
        for sh in range(1, kw):
            y = y + xpad_s[pad - sh:pad - sh + seq, :] * w_ref[kw - 1 - sh:kw - sh, ls]
        return _silu(y)

    def l2n(x):
        return x * lax.rsqrt(jnp.sum(x * x, axis=-1, keepdims=True) + EPS)

    prepared = []
    for g in range(heads):
        ls = slice(g * dh, (g + 1) * dh)
        head = group * heads + g
        q = l2n(conv_silu(q_ref[0, :, ls].astype(F32), cq_ref, ls)) * (dh ** -0.5)
        k = l2n(conv_silu(k_ref[0, :, ls].astype(F32), ck_ref, ls))
        v = conv_silu(v_ref[0, :, ls].astype(F32), cv_ref, ls)
        beta = jnp.broadcast_to(jnp.sum(jnp.where(lane == head, beta_all, 0.0), axis=-1, keepdims=True),
                                (seq, LANES))
        gcs = jnp.broadcast_to(jnp.sum(jnp.where(lane == head + C_HEADS, g_all, 0.0), axis=-1, keepdims=True),
                               (seq, LANES))
        sh = 1
        while sh < ck:
            gcs = gcs + jnp.where(rc >= sh, pltpu.roll(gcs, sh, axis=0), 0.0)
            sh *= 2
        q3, k3, v3 = (t.reshape(nc, ck, dh) for t in (q, k, v))
        beta3 = beta.reshape(nc, ck, LANES)
        gcs3 = gcs.reshape(nc, ck, LANES)
        egc3 = jnp.exp(gcs3)
        glast = gcs3[:, ck - 1:ck, :]
        gc_row = jnp.swapaxes(gcs3, 1, 2)[:, :ck, :]
        decay = jnp.exp(jnp.where(ii >= jj, gcs3[:, :, :ck] - gc_row, NEG))
        kb = k3.astype(BF16)
        qk = _bmm_nt(jnp.concatenate([q3.astype(BF16), kb], axis=1), kb)
        attn = (qk[:, :ck] * decay).astype(BF16)
        a_mat = jnp.where(ii > jj, beta3[:, :, :ck] * qk[:, ck:] * decay, 0.0)
        t = eye - jnp.where(_pair_mask(ii, jj, 1), a_mat, 0.0)
        s = 2
        while s < ck:
            tb = t.astype(BF16)
            x = _bmm(tb, jnp.where(_pair_mask(ii, jj, s), a_mat, 0.0).astype(BF16))
            t = t - _bmm(x.astype(BF16), tb)
            s *= 2
        rhs = jnp.concatenate([v3 * beta3, k3 * (beta3 * egc3)], axis=2).astype(BF16)
        uw = _bmm(t.astype(BF16), rhs)
        kd_t = jnp.swapaxes(k3 * jnp.exp(glast - gcs3), 1, 2).astype(BF16)
        pn = _bmm(kd_t, uw.astype(BF16))
        n_s[g] = pn[:, :, :dh]
        p_s[g] = pn[:, :, dh:].astype(BF16)
        egl_s[g] = jnp.exp(glast)
        wq = jnp.concatenate([uw[:, :, dh:], q3 * egc3], axis=1).astype(BF16)
        prepared.append((wq, attn, uw[:, :, :dh]))

    def scan(c, states):
        new = []
        for g in range(heads):
            sb = states[g].astype(BF16)
            sall_s[g, c] = sb
            new.append(states[g] * egl_s[g, c] + n_s[g, c] - _dot(p_s[g, c], sb))
        return tuple(new)

    lax.fori_loop(0, nc, scan, tuple(jnp.zeros((dh, dh), F32) for _ in range(heads)))

    for g in range(heads):
        ls = slice(g * dh, (g + 1) * dh)
        wq, attn, u = prepared[g]
        r = _bmm(wq, sall_s[g])
        v_new = u - r[:, :ck]
        o = (r[:, ck:] + _bmm(attn, v_new.astype(BF16))).reshape(seq, dh)
        gate = gate_ref[0, :, ls].astype(F32)
        o_ref[0, :, ls] = (_rms_rows(o, og_ref[...]) * _silu(gate)).astype(o_ref.dtype)


def delta_mix(proj, ba, conv_w, o_g, *, batch, seq, heads=2):
    nh = C_HEADS
    ngroup = nh // heads
    width = heads * C_HEAD_DIM
    nc = seq // C_CHUNK

    def slab(off):
        return pl.BlockSpec((1, seq, width), lambda b, h: (b, 0, off * ngroup + h))

    def conv_slab(off):
        return pl.BlockSpec((conv_w.shape[0], width), lambda b, h: (0, off * ngroup + h))

    return pl.pallas_call(
        functools.partial(_delta_kernel, heads=heads),
        grid=(batch, ngroup),
        in_specs=[slab(0), slab(1), slab(2), slab(3),
                  pl.BlockSpec((1, seq, LANES), lambda b, h: (b, 0, 0)),
                  conv_slab(0), conv_slab(1), conv_slab(2), _resident((1, LANES))],
        out_specs=pl.BlockSpec((1, seq, width), lambda b, h: (b, 0, h)),
        out_shape=jax.ShapeDtypeStruct((batch, seq, nh * C_HEAD_DIM), BF16),
        scratch_shapes=[pltpu.VMEM((heads, nc, C_HEAD_DIM, C_HEAD_DIM), BF16),
                        pltpu.VMEM((heads, nc, C_HEAD_DIM, C_HEAD_DIM), F32),
                        pltpu.VMEM((heads, nc, 1, LANES), F32),
                        pltpu.VMEM((heads, nc, C_HEAD_DIM, C_HEAD_DIM), BF16),
                        pltpu.VMEM((seq + SUBLANES, C_HEAD_DIM), F32)],
        compiler_params=_params("parallel", "parallel"),
        name="delta_mix",
    )(proj, proj, proj, proj, ba, conv_w, conv_w, conv_w, o_g.reshape(1, LANES))


def kernel(x, mem, norm_mix, norm_xattn, norm_mem, norm_ffn, ev_w_in, ev_conv, ev_q_norm, ev_k_norm, ev_w_out, od_w_in, od_conv, od_a_log, od_dt_bias, od_o_norm, od_w_out, xa_w_q, xa_w_kv, xa_q_norm, xa_k_norm, xa_w_o, ff_w_gu, ff_w_down, moe_router, moe_w_gu, moe_w_down):
    batch, seq, d = x.shape
    n = batch * seq
    h = x.reshape(n, d)
    depth = norm_mix.shape[0]
    for layer in range(depth):
        i = layer // 2
        g_mix = norm_mix[layer].reshape(1, d)
        g_ffn = norm_ffn[layer].reshape(1, d)
        kv = mem_kv(mem, norm_mem[layer].reshape(1, d), xa_w_kv[layer].astype(BF16), xa_k_norm[layer])
        xattn_args = (norm_xattn[layer].reshape(1, d), xa_w_q[layer].astype(BF16), xa_q_norm[layer], kv,
                      xa_w_o[layer].astype(BF16))
        if layer % 2 == 0:
            proj = norm_proj(h, g_mix, ev_w_in[i].astype(BF16))
            a, o = ev_mix(proj.reshape(batch, seq, -1), ev_conv[i], ev_q_norm[i], ev_k_norm[i],
                          batch=batch, seq=seq)
            w_out = ev_w_out[i].astype(BF16)
            h = layer_tail(h, [a.reshape(n, -1), o.reshape(n, -1)], [w_out[:A_WIDTH], w_out[A_WIDTH:]],
                           *xattn_args, (g_ffn, ff_w_gu[i].astype(BF16), ff_w_down[i].astype(BF16)), seq=seq)
        else:
            n_main = 4 * C_HEADS * C_HEAD_DIM
            w_gates = _split_weight(od_w_in[i][:, n_main:])
            proj, ba = norm_proj(h, g_mix, od_w_in[i][:, :n_main].astype(BF16), w_gates, od_a_log[i],
                                 od_dt_bias[i])
            o = delta_mix(proj.reshape(batch, seq, -1), ba.reshape(batch, seq, -1), od_conv[i], od_o_norm[i],
                          batch=batch, seq=seq)
            h = layer_tail(h, [o.reshape(n, -1)], [od_w_out[i].astype(BF16)], *xattn_args, seq=seq)
            h = moe(h, g_ffn, moe_router[i], moe_w_gu[i].astype(BF16), moe_w_down[i].astype(BF16))
    return h.reshape(batch, seq, d)
```

```python
import functools

import jax
import jax.numpy as jnp
from jax import lax
from jax.experimental import pallas as pl
from jax.experimental.pallas import tpu as pltpu

F32 = jnp.float32
BF16 = jnp.bfloat16

EPS = 1e-6
D_MODEL = 1024
A_WIDTH = 512
B_HEADS = 8
B_HEAD_DIM = 64
DILATED_PATTERN = ((128, 1), (512, 4), (2048, 16))
C_HEADS = 8
C_HEAD_DIM = 128
C_CHUNK = 64
X_HEADS = 4
X_HEAD_DIM = 64
N_EXPERTS = 8

LANES = 128
SUBLANES = 8
VMEM_LIMIT = 56 * 1024 * 1024
NEG = -1e30
LOG2_E = 1.4426950408889634


def _params(*sem):
    return pltpu.CompilerParams(dimension_semantics=sem, vmem_limit_bytes=VMEM_LIMIT)


def _resident(shape):
    nd = len(shape)
    return pl.BlockSpec(shape, lambda *_: (0,) * nd, pipeline_mode=pl.Buffered(1))


def _rms_rows(x, g):
    ms = jnp.mean(x * x, axis=-1, keepdims=True)
    return x * lax.rsqrt(ms + EPS) * g


def _silu(x):
    return x * jax.nn.sigmoid(x)


def _dot(a, b):
    return jnp.dot(a, b, preferred_element_type=F32)


def _dot_nt(a, b):
    return lax.dot_general(a, b, (((1,), (1,)), ((), ())), preferred_element_type=F32)


def _dot_tn(a, b):
    return lax.dot_general(a, b, (((0,), (0,)), ((), ())), preferred_element_type=F32)


def _head_inv_rms(x, head_dim):
    width = x.shape[-1]
    lane = lax.broadcasted_iota(jnp.int32, (1, width), 1)
    x2 = x * x
    inv = jnp.zeros_like(x)
    for h in range(width // head_dim):
        sel = (lane >= h * head_dim) & (lane < (h + 1) * head_dim)
        ms = jnp.sum(jnp.where(sel, x2, 0.0), axis=-1, keepdims=True) * (1.0 / head_dim)
        inv = jnp.where(sel, lax.rsqrt(ms + EPS), inv)
    return inv


def _norm_proj_kernel(x_ref, g_ref, w_ref, o_ref, *, chunk):
    xn = _rms_rows(x_ref[...], g_ref[...]).astype(BF16)
    for c in range(w_ref.shape[1] // chunk):
        cs = slice(c * chunk, (c + 1) * chunk)
        o_ref[:, cs] = _dot(xn, w_ref[:, cs]).astype(o_ref.dtype)


def _norm_proj_gates_kernel(x_ref, g_ref, w_ref, wg_ref, alog_ref, dtb_ref, o_ref, og_ref, *, chunk):
    xf = _rms_rows(x_ref[...], g_ref[...])
    xn = xf.astype(BF16)
    for c in range(w_ref.shape[1] // chunk):
        cs = slice(c * chunk, (c + 1) * chunk)
        o_ref[:, cs] = _dot(xn, w_ref[:, cs]).astype(o_ref.dtype)
    ba = _dot_split(xf, wg_ref)
    lane = lax.broadcasted_iota(jnp.int32, (1, LANES), 1)
    z = ba + dtb_ref[...]
    softplus = jnp.maximum(z, 0.0) + jnp.log(1.0 + jnp.exp(-jnp.abs(z)))
    og_ref[...] = jnp.where(lane < C_HEADS, jax.nn.sigmoid(ba), -jnp.exp(alog_ref[...]) * softplus)


def norm_proj(x, g, w, w_gates=None, a_log=None, dt_bias=None, *, tm=512, chunk=512):
    n, d = x.shape
    nout = w.shape[1]
    in_specs = [pl.BlockSpec((tm, d), lambda i: (i, 0)), _resident((1, d)), _resident((d, nout))]
    if w_gates is None:
        return pl.pallas_call(
            functools.partial(_norm_proj_kernel, chunk=chunk),
            grid=(n // tm,),
            in_specs=in_specs,
            out_specs=pl.BlockSpec((tm, nout), lambda i: (i, 0)),
            out_shape=jax.ShapeDtypeStruct((n, nout), BF16),
            compiler_params=_params("parallel"),
            name="norm_proj",
        )(x, g, w)
    ng = w_gates.shape[1] // 2
    nh = a_log.shape[0]
    alog = jnp.pad(a_log.reshape(1, nh), ((0, 0), (nh, ng - 2 * nh)))
    dtb = jnp.pad(dt_bias.reshape(1, nh), ((0, 0), (nh, ng - 2 * nh)))
    return pl.pallas_call(
        functools.partial(_norm_proj_gates_kernel, chunk=chunk),
        grid=(n // tm,),
        in_specs=in_specs + [_resident(w_gates.shape), _resident((1, ng)), _resident((1, ng))],
        out_specs=[pl.BlockSpec((tm, nout), lambda i: (i, 0)), pl.BlockSpec((tm, ng), lambda i: (i, 0))],
        out_shape=[jax.ShapeDtypeStruct((n, nout), BF16), jax.ShapeDtypeStruct((n, ng), F32)],
        compiler_params=_params("parallel"),
        name="norm_proj_gates",
    )(x, g, w, w_gates, alog, dtb)


def _dilation_bias(seq, blk):
    dist = (seq - blk) + jnp.arange(blk)[:, None] - jnp.arange(seq)[None, :]
    mult = jnp.zeros(dist.shape, F32)
    for window, d in DILATED_PATTERN:
        mult = mult + ((dist >= 0) & (dist % d == 0) & (dist <= window)).astype(F32)
    return jnp.where(mult > 0, jnp.log2(jnp.maximum(mult, 1.0)), NEG)


def _ev_mix_kernel(gb_ref, gc_ref, xc_ref, q_ref, k_ref, v_ref, cw_ref, qg_ref, kg_ref, bias_ref,
                   a_ref, o_ref, kn_ref, qh_ref, vh_ref, s_ref, *, blk, kchunk):
    seq = q_ref.shape[1]
    y = gc_ref[0].astype(F32) * xc_ref[0].astype(F32)
    row = lax.broadcasted_iota(jnp.int32, (seq, 1), 0)
    kw = cw_ref.shape[0]
    conv = y * cw_ref[kw - 1:kw, :]
    for sh in range(1, kw):
        ys = jnp.where(row >= sh, pltpu.roll(y, sh, axis=0), 0.0)
        conv = conv + ys * cw_ref[kw - 1 - sh:kw - sh, :]
    a_ref[0] = (gb_ref[0].astype(F32) * conv).astype(a_ref.dtype)

    lane = lax.broadcasted_iota(jnp.int32, (1, LANES), 1)
    first = lane < B_HEAD_DIM
    kf = k_ref[0].astype(F32)
    kn_ref[...] = (kf * _head_inv_rms(kf, B_HEAD_DIM) * kg_ref[...]).astype(BF16)
    scale = B_HEAD_DIM ** -0.5 * LOG2_E
    qf = q_ref[0].astype(F32)
    qn = qf * _head_inv_rms(qf, B_HEAD_DIM) * (qg_ref[...] * scale)
    vb = v_ref[0]
    one = jnp.ones((), BF16)
    for h in range(2):
        own = first if h == 0 else jnp.logical_not(first)
        qh_ref[h] = jnp.where(own, qn, 0.0).astype(BF16)
        vh_ref[h] = jnp.where(own, vb, one)

    def lane_groups(t):
        return [t[:, c:c + LANES] for c in range(0, t.shape[1], LANES)]

    def chunks_of(qi):
        kv_len = (qi + 1) * blk
        return [(lo, min(lo + kchunk, kv_len)) for lo in range(0, kv_len, kchunk)]

    def score_pass(qi):
        r0 = qi * blk
        ms = []
        for h in range(2):
            qh = qh_ref[h, r0:r0 + blk, :]
            mv = None
            for lo, hi in chunks_of(qi):
                c0 = lo + seq - blk - r0
                s = _dot_nt(qh, kn_ref[lo:hi, :]) + bias_ref[:, c0:c0 + hi - lo]
                s_ref[qi % 2, h, :, lo:hi] = s
                for part in lane_groups(s.astype(BF16)):
                    mv = part if mv is None else jnp.maximum(mv, part)
            ms.append(jnp.max(mv.astype(F32), axis=-1, keepdims=True))
        return ms

    def value_pass(qi, ms):
        r0 = qi * blk
        outs = []
        for h in range(2):
            acc = jnp.zeros((blk, LANES), F32)
            for lo, hi in chunks_of(qi):
                p = jnp.exp2(s_ref[qi % 2, h, :, lo:hi] - ms[h])
                acc = acc + _dot(p.astype(BF16), vh_ref[h, lo:hi, :])
            other = jnp.logical_not(first) if h == 0 else first
            l = jnp.sum(jnp.where(other, acc, 0.0), axis=-1, keepdims=True) * (1.0 / B_HEAD_DIM)
            outs.append(acc / l)
        o_ref[0, r0:r0 + blk, :] = jnp.where(first, outs[0], outs[1]).astype(o_ref.dtype)

    nq = seq // blk
    pending = score_pass(0)
    for qi in range(1, nq):
        ahead = score_pass(qi)
        value_pass(qi - 1, pending)
        pending = ahead
    value_pass(nq - 1, pending)


def ev_mix(proj, conv_w, q_g, k_g, *, batch, seq, blk=256, kchunk=256):
    nslab = A_WIDTH // LANES
    bias = _dilation_bias(seq, blk)
    qg2 = jnp.tile(q_g.reshape(1, -1), (1, LANES // B_HEAD_DIM))
    kg2 = jnp.tile(k_g.reshape(1, -1), (1, LANES // B_HEAD_DIM))

    def slab(off):
        return pl.BlockSpec((1, seq, LANES), lambda b, s: (b, 0, off * nslab + s))

    out_spec = pl.BlockSpec((1, seq, LANES), lambda b, s: (b, 0, s))
    return pl.pallas_call(
        functools.partial(_ev_mix_kernel, blk=blk, kchunk=kchunk),
        grid=(batch, nslab),
        in_specs=[slab(0), slab(1), slab(2), slab(3), slab(4), slab(5),
                  pl.BlockSpec((conv_w.shape[0], LANES), lambda b, s: (0, s)),
                  _resident((1, LANES)), _resident((1, LANES)), _resident(bias.shape)],
        out_specs=[out_spec, out_spec],
        out_shape=[jax.ShapeDtypeStruct((batch, seq, A_WIDTH), BF16)] * 2,
        scratch_shapes=[pltpu.VMEM((seq, LANES), BF16), pltpu.VMEM((2, seq, LANES), BF16),
                        pltpu.VMEM((2, seq, LANES), BF16), pltpu.VMEM((2, 2, blk, seq), F32)],
        compiler_params=_params("parallel", "parallel"),
        name="ev_mix",
    )(proj, proj, proj, proj, proj, proj, conv_w, qg2, kg2, bias)


def _mem_kv_kernel(m_ref, g_ref, w_ref, kg_ref, o_ref):
    width = X_HEADS * X_HEAD_DIM
    mn = _rms_rows(m_ref[0], g_ref[...]).astype(BF16)
    kv = _dot(mn, w_ref[...])
    k = kv[:, :width]
    o_ref[0, :, :width] = (k * _head_inv_rms(k, X_HEAD_DIM) * kg_ref[...]).astype(o_ref.dtype)
    o_ref[0, :, width:] = kv[:, width:].astype(o_ref.dtype)


def mem_kv(mem, g, w_kv, k_g):
    b, m, d = mem.shape
    width = w_kv.shape[1]
    kg = jnp.tile(k_g.reshape(1, -1), (1, X_HEADS))
    return pl.pallas_call(
        _mem_kv_kernel,
        grid=(b,),
        in_specs=[pl.BlockSpec((1, m, d), lambda i: (i, 0, 0)), _resident((1, d)), _resident(w_kv.shape),
                  _resident(kg.shape)],
        out_specs=pl.BlockSpec((1, m, width), lambda i: (i, 0, 0)),
        out_shape=jax.ShapeDtypeStruct((b, m, width), BF16),
        compiler_params=_params("parallel"),
        name="mem_kv",
    )(mem, g, w_kv, kg)


def _xattn_update(x, g_ref, wq_ref, qg_ref, kv_ref, wo_ref):
    width = X_HEADS * X_HEAD_DIM
    u = _rms_rows(x, g_ref[...]).astype(BF16)
    q = _dot(u, wq_ref[...])
    q = q * _head_inv_rms(q, X_HEAD_DIM) * (qg_ref[...] * X_HEAD_DIM ** -0.5)
    k = kv_ref[0, :, :width]
    v = kv_ref[0, :, width:]
    lane = lax.broadcasted_iota(jnp.int32, (1, width), 1)
    out = jnp.zeros(q.shape, F32)
    for h in range(X_HEADS):
        sel = (lane >= h * X_HEAD_DIM) & (lane < (h + 1) * X_HEAD_DIM)
        s = _dot_nt(jnp.where(sel, q, 0.0).astype(BF16), k)
        p = jnp.exp(s - jnp.max(s, axis=-1, keepdims=True))
        oh = _dot(p.astype(BF16), v) / jnp.sum(p, axis=-1, keepdims=True)
        out = jnp.where(sel, oh, out)
    return x + _dot(out.astype(BF16), wo_ref[...])


def _swiglu_act(u, wgu_ref, act_ref, d_ff, chunk):
    for lo in range(0, d_ff, chunk):
        hi = min(lo + chunk, d_ff)
        gate = _dot(u, wgu_ref[:, lo:hi])
        up = _dot(u, wgu_ref[:, d_ff + lo:d_ff + hi])
        act_ref[:, lo:hi] = (_silu(gate) * up).astype(act_ref.dtype)


def _tail_kernel(*refs, n_in, with_ffn, chunk):
    h_ref = refs[0]
    xs = refs[1:1 + n_in]
    ws = refs[1 + n_in:1 + 2 * n_in]
    xattn_refs = refs[1 + 2 * n_in:6 + 2 * n_in]
    x = h_ref[...]
    for x_ref, w_ref in zip(xs, ws):
        x = x + _dot(x_ref[...], w_ref[...])
    x = _xattn_update(x, *xattn_refs)
    if with_ffn:
        gf_ref, wgu_ref, wd_ref, o_ref, act_ref = refs[6 + 2 * n_in:]
        u = _rms_rows(x, gf_ref[...]).astype(BF16)
        _swiglu_act(u, wgu_ref, act_ref, wd_ref.shape[0], chunk)
        x = x + _dot(act_ref[...], wd_ref[...])
    else:
        o_ref = refs[-1]
    o_ref[...] = x


def layer_tail(h, xs, ws, g_x, w_q, q_g, kv, w_o, ffn_weights=None, *, seq, tm=512, chunk=256):
    n, d = h.shape
    qg = jnp.tile(q_g.reshape(1, -1), (1, X_HEADS))
    per_seq = seq // tm
    row = lambda i: (i, 0)
    in_specs = [pl.BlockSpec((tm, d), row)]
    in_specs += [pl.BlockSpec((tm, x.shape[1]), row) for x in xs]
    in_specs += [_resident(w.shape) for w in ws]
    in_specs += [_resident((1, d)), _resident(w_q.shape), _resident(qg.shape),
                 pl.BlockSpec((1,) + kv.shape[1:], lambda i: (i // per_seq, 0, 0)), _resident(w_o.shape)]
    args = [h, *xs, *ws, g_x, w_q, qg, kv, w_o]
    scratch = []
    if ffn_weights is not None:
        g_f, w_gu, w_down = ffn_weights
        in_specs += [_resident((1, d)), _resident(w_gu.shape), _resident(w_down.shape)]
        args += [g_f, w_gu, w_down]
        scratch = [pltpu.VMEM((tm, w_down.shape[0]), BF16)]
    return pl.pallas_call(
        functools.partial(_tail_kernel, n_in=len(xs), with_ffn=ffn_weights is not None, chunk=chunk),
        grid=(n // tm,),
        in_specs=in_specs,
        out_specs=pl.BlockSpec((tm, d), row),
        out_shape=jax.ShapeDtypeStruct((n, d), F32),
        scratch_shapes=scratch,
        compiler_params=_params("parallel"),
        name="layer_tail",
    )(*args)


def _split_bf16(x):
    hi = x.astype(BF16)
    return hi, (x - hi.astype(F32)).astype(BF16)


def _dot_split(x, w2_ref):
    half = w2_ref.shape[1] // 2
    xh, xl = _split_bf16(x)
    both = _dot(xh, w2_ref[...])
    return both[:, :half] + both[:, half:] + _dot(xl, w2_ref[:, :half])


def _split_weight(w):
    w = jnp.pad(w, ((0, 0), (0, LANES - w.shape[1])))
    hi = w.astype(BF16)
    return jnp.concatenate([hi, (w - hi.astype(F32)).astype(BF16)], axis=1)


def _moe_kernel(h_ref, g_ref, r2_ref, tri_ref, wgu_ref, wd_ref, o_ref,
                u_ref, rank_t_ref, wts_t_ref, act_ref, cnt_ref, *, chunk, sizes, sub_t):
    e = pl.program_id(1)
    n_sub = h_ref.shape[0] // sub_t
    d_ff = wd_ref.shape[1]
    lane = lax.broadcasted_iota(jnp.int32, (1, LANES), 1)

    @pl.when(e == 0)
    def _():
        x = h_ref[...]
        uf = _rms_rows(x, g_ref[...])
        u_ref[...] = uf.astype(BF16)
        logits = _dot_split(uf, r2_ref)
        logits = jnp.where(lane < N_EXPERTS, logits, NEG)
        m1 = jnp.max(logits, axis=-1, keepdims=True)
        i1 = jnp.min(jnp.where(logits == m1, lane, LANES), axis=-1, keepdims=True)
        rest = jnp.where(lane == i1, NEG, logits)
        m2 = jnp.max(rest, axis=-1, keepdims=True)
        i2 = jnp.min(jnp.where(rest == m2, lane, LANES), axis=-1, keepdims=True)
        e2 = jnp.exp(m2 - m1)
        w1 = 1.0 / (1.0 + e2)
        wts = jnp.where(lane == i1, w1, 0.0) + jnp.where(lane == i2, e2 * w1, 0.0)
        sel = jnp.where(lane == i1, 1.0, jnp.where(lane == i2, 1.0, 0.0))
        for s in range(n_sub):
            rs = slice(s * sub_t, (s + 1) * sub_t)
            sel_s = sel[rs]
            rank = jnp.where(sel_s > 0.0, _dot(tri_ref[...], sel_s.astype(BF16)), -1.0)
            rank_t_ref[s] = jnp.transpose(rank)
            wts_t_ref[s] = jnp.transpose(wts[rs])
            counts = jnp.sum(sel_s, axis=0, keepdims=True)
            for ee in range(N_EXPERTS):
                cnt_ref[s * N_EXPERTS + ee] = jnp.sum(jnp.where(lane == ee, counts, 0.0)).astype(jnp.int32)
        o_ref[...] = x

    def run_block(first, size, rs, rank_row, w_row):
        sub = lax.broadcasted_iota(jnp.int32, (size, 1), 0).astype(F32)
        act = act_ref.at[pl.ds(0, size)]
        hit = rank_row == first.astype(F32) + sub
        pick = jnp.where(hit, 1.0, 0.0).astype(BF16)
        w_rows = jnp.sum(jnp.where(hit, w_row, 0.0), axis=-1, keepdims=True)
        xg = _dot(pick, u_ref[rs, :]).astype(BF16)
        _swiglu_act(xg, wgu_ref.at[0], act, d_ff, chunk)
        y = (_dot(act[...], wd_ref[0]) * w_rows).astype(BF16)
        o_ref[rs, :] += _dot_tn(pick, y)

    big = sizes[-1]
    for s in range(n_sub):
        rs = slice(s * sub_t, (s + 1) * sub_t)
        args = (rs, rank_t_ref[s, pl.ds(e, 1), :], wts_t_ref[s, pl.ds(e, 1), :])
        count = cnt_ref[s * N_EXPERTS + e]
        n_full = count // big
        rem = count - n_full * big

        def full_block(b, carry, args=args):
            run_block(b * big, big, *args)
            return carry

        lax.fori_loop(0, n_full + (rem > sizes[-2]).astype(jnp.int32), full_block, 0)
        below = 0
        for size in sizes[:-1]:
            @pl.when(jnp.logical_and(rem > below, rem <= size))
            def _(size=size, args=args, n_full=n_full):
                run_block(n_full * big, size, *args)
            below = size


def moe(h, g, router, w_gu, w_down, *, tm=1024, sub_t=512, chunk=512, sizes=(128, 160, 192)):
    n, d = h.shape
    n_exp, d_ff, _ = w_down.shape
    n_sub = tm // sub_t
    tri = jnp.tril(jnp.ones((sub_t, sub_t), BF16), -1)
    r2 = _split_weight(router)
    return pl.pallas_call(
        functools.partial(_moe_kernel, chunk=chunk, sizes=sizes, sub_t=sub_t),
        grid=(n // tm, n_exp),
        in_specs=[pl.BlockSpec((tm, d), lambda i, e: (i, 0)), _resident((1, d)), _resident(r2.shape),
                  _resident(tri.shape),
                  pl.BlockSpec((1, d, 2 * d_ff), lambda i, e: (e, 0, 0)),
                  pl.BlockSpec((1, d_ff, d), lambda i, e: (e, 0, 0))],
        out_specs=pl.BlockSpec((tm, d), lambda i, e: (i, 0)),
        out_shape=jax.ShapeDtypeStruct((n, d), F32),
        scratch_shapes=[pltpu.VMEM((tm, d), BF16), pltpu.VMEM((n_sub, LANES, sub_t), F32),
                        pltpu.VMEM((n_sub, LANES, sub_t), F32), pltpu.VMEM((sizes[-1], d_ff), BF16),
                        pltpu.SMEM((n_sub * N_EXPERTS,), jnp.int32)],
        compiler_params=_params("parallel", "arbitrary"),
        name="moe",
    )(h, g, r2, tri, w_gu, w_down)


def _pair_mask(ii, jj, s):
    bi = ii // s
    return (bi % 2 == 1) & (jj // s == bi - 1)


def _bmm(a, b):
    return lax.dot_general(a, b, (((2,), (1,)), ((0,), (0,))), preferred_element_type=F32)


def _bmm_nt(a, b):
    return lax.dot_general(a, b, (((2,), (2,)), ((0,), (0,))), preferred_element_type=F32)


def _delta_kernel(q_ref, k_ref, v_ref, gate_ref, ba_ref, cq_ref, ck_ref, cv_ref, og_ref,
                  o_ref, p_s, n_s, egl_s, sall_s, xpad_s, *, heads):
    seq = q_ref.shape[1]
    ck = C_CHUNK
    nc = seq // ck
    dh = C_HEAD_DIM
    group = pl.program_id(1)
    row = lax.broadcasted_iota(jnp.int32, (seq, 1), 0)
    rc = row % ck
    lane = lax.broadcasted_iota(jnp.int32, (1, LANES), 1)
    ii = lax.broadcasted_iota(jnp.int32, (1, ck, ck), 1)
    jj = lax.broadcasted_iota(jnp.int32, (1, ck, ck), 2)
    eye = (ii == jj).astype(F32)

    beta_all = ba_ref[0]
    g_all = beta_all

    pad = xpad_s.shape[0] - seq
    xpad_s[0:pad, :] = jnp.zeros((pad, dh), F32)

    def conv_silu(x, w_ref, ls):
        kw = w_ref.shape[0]
        xpad_s[pad:, :] = x
        y = x * w_ref[kw - 1:kw, ls]
        for sh in range(1, kw):
            y = y + xpad_s[pad - sh:pad - sh + seq, :] * w_ref[kw - 1 - sh:kw - sh, ls]
        return _silu(y)

    def l2n(x):
        return x * lax.rsqrt(jnp.sum(x * x, axis=-1, keepdims=True) + EPS)

    prepared = []
    for g in range(heads):
        ls = slice(g * dh, (g + 1) * dh)
        head = group * heads + g
        q = l2n(conv_silu(q_ref[0, :, ls].astype(F32), cq_ref, ls)) * (dh ** -0.5)
        k = l2n(conv_silu(k_ref[0, :, ls].astype(F32), ck_ref, ls))
        v = conv_silu(v_ref[0, :, ls].astype(F32), cv_ref, ls)
        beta = jnp.broadcast_to(jnp.sum(jnp.where(lane == head, beta_all, 0.0), axis=-1, keepdims=True),
                                (seq, LANES))
        gcs = jnp.broadcast_to(jnp.sum(jnp.where(lane == head + C_HEADS, g_all, 0.0), axis=-1, keepdims=True),
                               (seq, LANES))
        sh = 1
        while sh < ck:
            gcs = gcs + jnp.where(rc >= sh, pltpu.roll(gcs, sh, axis=0), 0.0)
            sh *= 2
        q3, k3, v3 = (t.reshape(nc, ck, dh) for t in (q, k, v))
        beta3 = beta.reshape(nc, ck, LANES)
        gcs3 = gcs.reshape(nc, ck, LANES)
        egc3 = jnp.exp(gcs3)
        glast = gcs3[:, ck - 1:ck, :]
        gc_row = jnp.swapaxes(gcs3, 1, 2)[:, :ck, :]
        decay = jnp.exp(jnp.where(ii >= jj, gcs3[:, :, :ck] - gc_row, NEG))
        kb = k3.astype(BF16)
        qk = _bmm_nt(jnp.concatenate([q3.astype(BF16), kb], axis=1), kb)
        attn = (qk[:, :ck] * decay).astype(BF16)
        a_mat = jnp.where(ii > jj, beta3[:, :, :ck] * qk[:, ck:] * decay, 0.0)
        t = eye - jnp.where(_pair_mask(ii, jj, 1), a_mat, 0.0)
        s = 2
        while s < ck:
            tb = t.astype(BF16)
            x = _bmm(tb, jnp.where(_pair_mask(ii, jj, s), a_mat, 0.0).astype(BF16))
            t = t - _bmm(x.astype(BF16), tb)
            s *= 2
        rhs = jnp.concatenate([v3 * beta3, k3 * (beta3 * egc3)], axis=2).astype(BF16)
        uw = _bmm(t.astype(BF16), rhs)
        kd_t = jnp.swapaxes(k3 * jnp.exp(glast - gcs3), 1, 2).astype(BF16)
        pn = _bmm(kd_t, uw.astype(BF16))
        n_s[g] = pn[:, :, :dh]
        p_s[g] = pn[:, :, dh:].astype(BF16)
        egl_s[g] = jnp.exp(glast)
        wq = jnp.concatenate([uw[:, :, dh:], q3 * egc3], axis=1).astype(BF16)
        prepared.append((wq, attn, uw[:, :, :dh]))

    def scan(c, states):
        new = []
        for g in range(heads):
            sb = states[g].astype(BF16)
            sall_s[g, c] = sb
            new.append(states[g] * egl_s[g, c] + n_s[g, c] - _dot(p_s[g, c], sb))
        return tuple(new)

    lax.fori_loop(0, nc, scan, tuple(jnp.zeros((dh, dh), F32) for _ in range(heads)))

    for g in range(heads):
        ls = slice(g * dh, (g + 1) * dh)
        wq, attn, u = prepared[g]
        r = _bmm(wq, sall_s[g])
        v_new = u - r[:, :ck]
        o = (r[:, ck:] + _bmm(attn, v_new.astype(BF16))).reshape(seq, dh)
        gate = gate_ref[0, :, ls].astype(F32)
        o_ref[0, :, ls] = (_rms_rows(o, og_ref[...]) * _silu(gate)).astype(o_ref.dtype)


def delta_mix(proj, ba, conv_w, o_g, *, batch, seq, heads=2):
    nh = C_HEADS
    ngroup = nh // heads
    width = heads * C_HEAD_DIM
    nc = seq // C_CHUNK

    def slab(off):
        return pl.BlockSpec((1, seq, width), lambda b, h: (b, 0, off * ngroup + h))

    def conv_slab(off):
        return pl.BlockSpec((conv_w.shape[0], width), lambda b, h: (0, off * ngroup + h))

    return pl.pallas_call(
        functools.partial(_delta_kernel, heads=heads),
        grid=(batch, ngroup),
        in_specs=[slab(0), slab(1), slab(2), slab(3),
                  pl.BlockSpec((1, seq, LANES), lambda b, h: (b, 0, 0)),
                  conv_slab(0), conv_slab(1), conv_slab(2), _resident((1, LANES))],
        out_specs=pl.BlockSpec((1, seq, width), lambda b, h: (b, 0, h)),
        out_shape=jax.ShapeDtypeStruct((batch, seq, nh * C_HEAD_DIM), BF16),
        scratch_shapes=[pltpu.VMEM((heads, nc, C_HEAD_DIM, C_HEAD_DIM), BF16),
                        pltpu.VMEM((heads, nc, C_HEAD_DIM, C_HEAD_DIM), F32),
                        pltpu.VMEM((heads, nc, 1, LANES), F32),
                        pltpu.VMEM((heads, nc, C_HEAD_DIM, C_HEAD_DIM), BF16),
                        pltpu.VMEM((seq + SUBLANES, C_HEAD_DIM), F32)],
        compiler_params=_params("parallel", "parallel"),
        name="delta_mix",
    )(proj, proj, proj, proj, ba, conv_w, conv_w, conv_w, o_g.reshape(1, LANES))


def kernel(x, mem, norm_mix, norm_xattn, norm_mem, norm_ffn, ev_w_in, ev_conv, ev_q_norm, ev_k_norm, ev_w_out, od_w_in, od_conv, od_a_log, od_dt_bias, od_o_norm, od_w_out, xa_w_q, xa_w_kv, xa_q_norm, xa_k_norm, xa_w_o, ff_w_gu, ff_w_down, moe_router, moe_w_gu, moe_w_down):
    batch, seq, d = x.shape
    n = batch * seq
    h = x.reshape(n, d)
    depth = norm_mix.shape[0]
    for layer in range(depth):
        i = layer // 2
        g_mix = norm_mix[layer].reshape(1, d)
        g_ffn = norm_ffn[layer].reshape(1, d)
        kv = mem_kv(mem, norm_mem[layer].reshape(1, d), xa_w_kv[layer].astype(BF16), xa_k_norm[layer])
        xattn_args = (norm_xattn[layer].reshape(1, d), xa_w_q[layer].astype(BF16), xa_q_norm[layer], kv,
                      xa_w_o[layer].astype(BF16))
        if layer % 2 == 0:
            proj = norm_proj(h, g_mix, ev_w_in[i].astype(BF16))
            a, o = ev_mix(proj.reshape(batch, seq, -1), ev_conv[i], ev_q_norm[i], ev_k_norm[i],
                          batch=batch, seq=seq)
            w_out = ev_w_out[i].astype(BF16)
            h = layer_tail(h, [a.reshape(n, -1), o.reshape(n, -1)], [w_out[:A_WIDTH], w_out[A_WIDTH:]],
                           *xattn_args, (g_ffn, ff_w_gu[i].astype(BF16), ff_w_down[i].astype(BF16)), seq=seq)
        else:
            n_main = 4 * C_HEADS * C_HEAD_DIM
            w_gates = _split_weight(od_w_in[i][:, n_main:])
            proj, ba = norm_proj(h, g_mix, od_w_in[i][:, :n_main].astype(BF16), w_gates, od_a_log[i],
                                 od_dt_bias[i])
            o = delta_mix(proj.reshape(batch, seq, -1), ba.reshape(batch, seq, -1), od_conv[i], od_o_norm[i],
                          batch=batch, seq=seq)
            h = layer_tail(h, [o.reshape(n, -1)], [od_w_out[i].astype(BF16)], *xattn_args, seq=seq)
            h = moe(h, g_ffn, moe_router[i], moe_w_gu[i].astype(BF16), moe_w_down[i].astype(BF16))
    return h.reshape(batch, seq, d)
```

```python
import functools

import jax
import jax.numpy as jnp
from jax import lax
from jax.experimental import pallas as pl
from jax.experimental.pallas import tpu as pltpu

F32 = jnp.float32
BF16 = jnp.bfloat16

EPS = 1e-6
D_MODEL = 1024
A_WIDTH = 512
B_HEADS = 8
B_HEAD_DIM = 64
DILATED_PATTERN = ((128, 1), (512, 4), (2048, 16))
C_HEADS = 8
C_HEAD_DIM = 128
C_CHUNK = 64
X_HEADS = 4
X_HEAD_DIM = 64
N_EXPERTS = 8

LANES = 128
SUBLANES = 8
VMEM_LIMIT = 56 * 1024 * 1024
NEG = -1e30
LOG2_E = 1.4426950408889634


def _params(*sem):
    return pltpu.CompilerParams(dimension_semantics=sem, vmem_limit_bytes=VMEM_LIMIT)


def _resident(shape):
    nd = len(shape)
    return pl.BlockSpec(shape, lambda *_: (0,) * nd, pipeline_mode=pl.Buffered(1))


def _rms_rows(x, g):
    ms = jnp.mean(x * x, axis=-1, keepdims=True)
    return x * lax.rsqrt(ms + EPS) * g


def _silu(x):
    return x * jax.nn.sigmoid(x)


def _dot(a, b):
    return jnp.dot(a, b, preferred_element_type=F32)


def _dot_nt(a, b):
    return lax.dot_general(a, b, (((1,), (1,)), ((), ())), preferred_element_type=F32)


def _dot_tn(a, b):
    return lax.dot_general(a, b, (((0,), (0,)), ((), ())), preferred_element_type=F32)


def _head_inv_rms(x, head_dim):
    width = x.shape[-1]
    lane = lax.broadcasted_iota(jnp.int32, (1, width), 1)
    x2 = x * x
    inv = jnp.zeros_like(x)
    for h in range(width // head_dim):
        sel = (lane >= h * head_dim) & (lane < (h + 1) * head_dim)
        ms = jnp.sum(jnp.where(sel, x2, 0.0), axis=-1, keepdims=True) * (1.0 / head_dim)
        inv = jnp.where(sel, lax.rsqrt(ms + EPS), inv)
    return inv


def _norm_proj_kernel(x_ref, g_ref, w_ref, o_ref, *, chunk):
    xn = _rms_rows(x_ref[...], g_ref[...]).astype(BF16)
    for c in range(w_ref.shape[1] // chunk):
        cs = slice(c * chunk, (c + 1) * chunk)
        o_ref[:, cs] = _dot(xn, w_ref[:, cs]).astype(o_ref.dtype)


def _norm_proj_gates_kernel(x_ref, g_ref, w_ref, wg_ref, alog_ref, dtb_ref, o_ref, og_ref, *, chunk):
    xf = _rms_rows(x_ref[...], g_ref[...])
    xn = xf.astype(BF16)
    for c in range(w_ref.shape[1] // chunk):
        cs = slice(c * chunk, (c + 1) * chunk)
        o_ref[:, cs] = _dot(xn, w_ref[:, cs]).astype(o_ref.dtype)
    ba = _dot_split(xf, wg_ref)
    lane = lax.broadcasted_iota(jnp.int32, (1, LANES), 1)
    z = ba + dtb_ref[...]
    softplus = jnp.maximum(z, 0.0) + jnp.log(1.0 + jnp.exp(-jnp.abs(z)))
    cum = -jnp.exp(alog_ref[...]) * softplus
    in_chunk = lax.broadcasted_iota(jnp.int32, (ba.shape[0], 1), 0) % C_CHUNK
    sh = 1
    while sh < C_CHUNK:
        cum = cum + jnp.where(in_chunk >= sh, pltpu.roll(cum, sh, axis=0), 0.0)
        sh *= 2
    og_ref[...] = jnp.where(lane < C_HEADS, jax.nn.sigmoid(ba), cum)


def norm_proj(x, g, w, w_gates=None, a_log=None, dt_bias=None, *, tm=1024, chunk=512):
    n, d = x.shape
    nout = w.shape[1]
    in_specs = [pl.BlockSpec((tm, d), lambda i: (i, 0)), _resident((1, d)), _resident((d, nout))]
    if w_gates is None:
        return pl.pallas_call(
            functools.partial(_norm_proj_kernel, chunk=chunk),
            grid=(n // tm,),
            in_specs=in_specs,
            out_specs=pl.BlockSpec((tm, nout), lambda i: (i, 0)),
            out_shape=jax.ShapeDtypeStruct((n, nout), BF16),
            compiler_params=_params("parallel"),
            name="norm_proj",
        )(x, g, w)
    ng = w_gates.shape[1] // 2
    nh = a_log.shape[0]
    alog = jnp.pad(a_log.reshape(1, nh), ((0, 0), (nh, ng - 2 * nh)))
    dtb = jnp.pad(dt_bias.reshape(1, nh), ((0, 0), (nh, ng - 2 * nh)))
    return pl.pallas_call(
        functools.partial(_norm_proj_gates_kernel, chunk=chunk),
        grid=(n // tm,),
        in_specs=in_specs + [_resident(w_gates.shape), _resident((1, ng)), _resident((1, ng))],
        out_specs=[pl.BlockSpec((tm, nout), lambda i: (i, 0)), pl.BlockSpec((tm, ng), lambda i: (i, 0))],
        out_shape=[jax.ShapeDtypeStruct((n, nout), BF16), jax.ShapeDtypeStruct((n, ng), F32)],
        compiler_params=_params("parallel"),
        name="norm_proj_gates",
    )(x, g, w, w_gates, alog, dtb)


def _dilation_bias(seq, blk):
    dist = (seq - blk) + jnp.arange(blk)[:, None] - jnp.arange(seq)[None, :]
    mult = jnp.zeros(dist.shape, F32)
    for window, d in DILATED_PATTERN:
        mult = mult + ((dist >= 0) & (dist % d == 0) & (dist <= window)).astype(F32)
    return jnp.where(mult > 0, jnp.log2(jnp.maximum(mult, 1.0)), NEG)


def _ev_mix_kernel(gb_ref, gc_ref, xc_ref, q_ref, k_ref, v_ref, cw_ref, qg_ref, kg_ref, bias_ref,
                   a_ref, o_ref, kn_ref, qh_ref, vh_ref, s_ref, *, blk, kchunk):
    seq = q_ref.shape[1]
    y = gc_ref[0].astype(F32) * xc_ref[0].astype(F32)
    row = lax.broadcasted_iota(jnp.int32, (seq, 1), 0)
    kw = cw_ref.shape[0]
    conv = y * cw_ref[kw - 1:kw, :]
    for sh in range(1, kw):
        ys = jnp.where(row >= sh, pltpu.roll(y, sh, axis=0), 0.0)
        conv = conv + ys * cw_ref[kw - 1 - sh:kw - sh, :]
    a_ref[0] = (gb_ref[0].astype(F32) * conv).astype(a_ref.dtype)

    lane = lax.broadcasted_iota(jnp.int32, (1, LANES), 1)
    first = lane < B_HEAD_DIM
    kf = k_ref[0].astype(F32)
    kn_ref[...] = (kf * _head_inv_rms(kf, B_HEAD_DIM) * kg_ref[...]).astype(BF16)
    scale = B_HEAD_DIM ** -0.5 * LOG2_E
    qf = q_ref[0].astype(F32)
    qn = qf * _head_inv_rms(qf, B_HEAD_DIM) * (qg_ref[...] * scale)
    vb = v_ref[0]
    one = jnp.ones((), BF16)
    for h in range(2):
        own = first if h == 0 else jnp.logical_not(first)
        qh_ref[h] = jnp.where(own, qn, 0.0).astype(BF16)
        vh_ref[h] = jnp.where(own, vb, one)

    def lane_groups(t):
        return [t[:, c:c + LANES] for c in range(0, t.shape[1], LANES)]

    def chunks_of(qi):
        kv_len = (qi + 1) * blk
        return [(lo, min(lo + kchunk, kv_len)) for lo in range(0, kv_len, kchunk)]

    def score_pass(qi):
        r0 = qi * blk
        ms = []
        for h in range(2):
            qh = qh_ref[h, r0:r0 + blk, :]
            mv = None
            for lo, hi in chunks_of(qi):
                c0 = lo + seq - blk - r0
                s = _dot_nt(qh, kn_ref[lo:hi, :]) + bias_ref[:, c0:c0 + hi - lo]
                s_ref[qi % 2, h, :, lo:hi] = s
                for part in lane_groups(s.astype(BF16)):
                    mv = part if mv is None else jnp.maximum(mv, part)
            ms.append(jnp.max(mv.astype(F32), axis=-1, keepdims=True))
        return ms

    def value_pass(qi, ms):
        r0 = qi * blk
        outs = []
        for h in range(2):
            acc = jnp.zeros((blk, LANES), F32)
            for lo, hi in chunks_of(qi):
                p = jnp.exp2(s_ref[qi % 2, h, :, lo:hi] - ms[h])
                acc = acc + _dot(p.astype(BF16), vh_ref[h, lo:hi, :])
            other = jnp.logical_not(first) if h == 0 else first
            l = jnp.sum(jnp.where(other, acc, 0.0), axis=-1, keepdims=True) * (1.0 / B_HEAD_DIM)
            outs.append(acc / l)
        o_ref[0, r0:r0 + blk, :] = jnp.where(first, outs[0], outs[1]).astype(o_ref.dtype)

    nq = seq // blk
    pending = score_pass(0)
    for qi in range(1, nq):
        ahead = score_pass(qi)
        value_pass(qi - 1, pending)
        pending = ahead
    value_pass(nq - 1, pending)


def ev_mix(proj, conv_w, q_g, k_g, *, batch, seq, blk=256, kchunk=256):
    nslab = A_WIDTH // LANES
    bias = _dilation_bias(seq, blk)
    qg2 = jnp.tile(q_g.reshape(1, -1), (1, LANES // B_HEAD_DIM))
    kg2 = jnp.tile(k_g.reshape(1, -1), (1, LANES // B_HEAD_DIM))

    def slab(off):
        return pl.BlockSpec((1, seq, LANES), lambda b, s: (b, 0, off * nslab + s))

    out_spec = pl.BlockSpec((1, seq, LANES), lambda b, s: (b, 0, s))
    return pl.pallas_call(
        functools.partial(_ev_mix_kernel, blk=blk, kchunk=kchunk),
        grid=(batch, nslab),
        in_specs=[slab(0), slab(1), slab(2), slab(3), slab(4), slab(5),
                  pl.BlockSpec((conv_w.shape[0], LANES), lambda b, s: (0, s)),
                  _resident((1, LANES)), _resident((1, LANES)), _resident(bias.shape)],
        out_specs=[out_spec, out_spec],
        out_shape=[jax.ShapeDtypeStruct((batch, seq, A_WIDTH), BF16)] * 2,
        scratch_shapes=[pltpu.VMEM((seq, LANES), BF16), pltpu.VMEM((2, seq, LANES), BF16),
                        pltpu.VMEM((2, seq, LANES), BF16), pltpu.VMEM((2, 2, blk, seq), F32)],
        compiler_params=_params("parallel", "parallel"),
        name="ev_mix",
    )(proj, proj, proj, proj, proj, proj, conv_w, qg2, kg2, bias)


def _mem_kv_kernel(m_ref, g_ref, w_ref, kg_ref, o_ref):
    width = X_HEADS * X_HEAD_DIM
    mn = _rms_rows(m_ref[0], g_ref[...]).astype(BF16)
    kv = _dot(mn, w_ref[...])
    k = kv[:, :width]
    o_ref[0, :, :width] = (k * _head_inv_rms(k, X_HEAD_DIM) * kg_ref[...]).astype(o_ref.dtype)
    o_ref[0, :, width:] = kv[:, width:].astype(o_ref.dtype)


def mem_kv(mem, g, w_kv, k_g):
    b, m, d = mem.shape
    width = w_kv.shape[1]
    kg = jnp.tile(k_g.reshape(1, -1), (1, X_HEADS))
    return pl.pallas_call(
        _mem_kv_kernel,
        grid=(b,),
        in_specs=[pl.BlockSpec((1, m, d), lambda i: (i, 0, 0)), _resident((1, d)), _resident(w_kv.shape),
                  _resident(kg.shape)],
        out_specs=pl.BlockSpec((1, m, width), lambda i: (i, 0, 0)),
        out_shape=jax.ShapeDtypeStruct((b, m, width), BF16),
        compiler_params=_params("parallel"),
        name="mem_kv",
    )(mem, g, w_kv, kg)


def _xattn_update(x, g_ref, wq_ref, qg_ref, kv_ref, wo_ref):
    width = X_HEADS * X_HEAD_DIM
    u = _rms_rows(x, g_ref[...]).astype(BF16)
    q = _dot(u, wq_ref[...])
    q = q * _head_inv_rms(q, X_HEAD_DIM) * (qg_ref[...] * X_HEAD_DIM ** -0.5)
    k = kv_ref[0, :, :width]
    v = kv_ref[0, :, width:]
    lane = lax.broadcasted_iota(jnp.int32, (1, width), 1)
    out = jnp.zeros(q.shape, F32)
    for h in range(X_HEADS):
        sel = (lane >= h * X_HEAD_DIM) & (lane < (h + 1) * X_HEAD_DIM)
        s = _dot_nt(jnp.where(sel, q, 0.0).astype(BF16), k)
        p = jnp.exp(s - jnp.max(s, axis=-1, keepdims=True))
        oh = _dot(p.astype(BF16), v) / jnp.sum(p, axis=-1, keepdims=True)
        out = jnp.where(sel, oh, out)
    return x + _dot(out.astype(BF16), wo_ref[...])


def _swiglu_act(u, wgu_ref, act_ref, d_ff, chunk):
    for lo in range(0, d_ff, chunk):
        hi = min(lo + chunk, d_ff)
        gate = _dot(u, wgu_ref[:, lo:hi])
        up = _dot(u, wgu_ref[:, d_ff + lo:d_ff + hi])
        act_ref[:, lo:hi] = (_silu(gate) * up).astype(act_ref.dtype)


def _tail_kernel(*refs, n_in, with_ffn, chunk):
    h_ref = refs[0]
    xs = refs[1:1 + n_in]
    ws = refs[1 + n_in:1 + 2 * n_in]
    xattn_refs = refs[1 + 2 * n_in:6 + 2 * n_in]
    x = h_ref[...]
    for x_ref, w_ref in zip(xs, ws):
        x = x + _dot(x_ref[...], w_ref[...])
    x = _xattn_update(x, *xattn_refs)
    if with_ffn:
        gf_ref, wgu_ref, wd_ref, o_ref, act_ref = refs[6 + 2 * n_in:]
        u = _rms_rows(x, gf_ref[...]).astype(BF16)
        _swiglu_act(u, wgu_ref, act_ref, wd_ref.shape[0], chunk)
        x = x + _dot(act_ref[...], wd_ref[...])
    else:
        o_ref = refs[-1]
    o_ref[...] = x


def layer_tail(h, xs, ws, g_x, w_q, q_g, kv, w_o, ffn_weights=None, *, seq, tm=512, chunk=256):
    n, d = h.shape
    qg = jnp.tile(q_g.reshape(1, -1), (1, X_HEADS))
    per_seq = seq // tm
    row = lambda i: (i, 0)
    in_specs = [pl.BlockSpec((tm, d), row)]
    in_specs += [pl.BlockSpec((tm, x.shape[1]), row) for x in xs]
    in_specs += [_resident(w.shape) for w in ws]
    in_specs += [_resident((1, d)), _resident(w_q.shape), _resident(qg.shape),
                 pl.BlockSpec((1,) + kv.shape[1:], lambda i: (i // per_seq, 0, 0)), _resident(w_o.shape)]
    args = [h, *xs, *ws, g_x, w_q, qg, kv, w_o]
    scratch = []
    if ffn_weights is not None:
        g_f, w_gu, w_down = ffn_weights
        in_specs += [_resident((1, d)), _resident(w_gu.shape), _resident(w_down.shape)]
        args += [g_f, w_gu, w_down]
        scratch = [pltpu.VMEM((tm, w_down.shape[0]), BF16)]
    return pl.pallas_call(
        functools.partial(_tail_kernel, n_in=len(xs), with_ffn=ffn_weights is not None, chunk=chunk),
        grid=(n // tm,),
        in_specs=in_specs,
        out_specs=pl.BlockSpec((tm, d), row),
        out_shape=jax.ShapeDtypeStruct((n, d), F32),
        scratch_shapes=scratch,
        compiler_params=_params("parallel"),
        name="layer_tail",
    )(*args)


def _split_bf16(x):
    hi = x.astype(BF16)
    return hi, (x - hi.astype(F32)).astype(BF16)


def _dot_split(x, w2_ref):
    half = w2_ref.shape[1] // 2
    xh, xl = _split_bf16(x)
    both = _dot(xh, w2_ref[...])
    return both[:, :half] + both[:, half:] + _dot(xl, w2_ref[:, :half])


def _split_weight(w):
    w = jnp.pad(w, ((0, 0), (0, LANES - w.shape[1])))
    hi = w.astype(BF16)
    return jnp.concatenate([hi, (w - hi.astype(F32)).astype(BF16)], axis=1)


def _moe_kernel(h_ref, g_ref, r2_ref, tri_ref, wgu_ref, wd_ref, o_ref,
                u_ref, rank_t_ref, wts_t_ref, act_ref, cnt_ref, *, chunk, sizes, sub_t):
    e = pl.program_id(1)
    n_sub = h_ref.shape[0] // sub_t
    d_ff = wd_ref.shape[1]
    lane = lax.broadcasted_iota(jnp.int32, (1, LANES), 1)

    @pl.when(e == 0)
    def _():
        x = h_ref[...]
        uf = _rms_rows(x, g_ref[...])
        u_ref[...] = uf.astype(BF16)
        logits = _dot_split(uf, r2_ref)
        logits = jnp.where(lane < N_EXPERTS, logits, NEG)
        m1 = jnp.max(logits, axis=-1, keepdims=True)
        i1 = jnp.min(jnp.where(logits == m1, lane, LANES), axis=-1, keepdims=True)
        rest = jnp.where(lane == i1, NEG, logits)
        m2 = jnp.max(rest, axis=-1, keepdims=True)
        i2 = jnp.min(jnp.where(rest == m2, lane, LANES), axis=-1, keepdims=True)
        e2 = jnp.exp(m2 - m1)
        w1 = 1.0 / (1.0 + e2)
        wts = jnp.where(lane == i1, w1, 0.0) + jnp.where(lane == i2, e2 * w1, 0.0)
        sel = jnp.where(lane == i1, 1.0, jnp.where(lane == i2, 1.0, 0.0))
        for s in range(n_sub):
            rs = slice(s * sub_t, (s + 1) * sub_t)
            sel_s = sel[rs]
            rank = jnp.where(sel_s > 0.0, _dot(tri_ref[...], sel_s.astype(BF16)), -1.0)
            rank_t_ref[s] = jnp.transpose(rank)
            wts_t_ref[s] = jnp.transpose(wts[rs])
            counts = jnp.sum(sel_s, axis=0, keepdims=True)
            for ee in range(N_EXPERTS):
                cnt_ref[s * N_EXPERTS + ee] = jnp.sum(jnp.where(lane == ee, counts, 0.0)).astype(jnp.int32)
        o_ref[...] = x

    def run_block(first, size, rs, rank_row, w_row):
        sub = lax.broadcasted_iota(jnp.int32, (size, 1), 0).astype(F32)
        act = act_ref.at[pl.ds(0, size)]
        hit = rank_row == first.astype(F32) + sub
        pick = jnp.where(hit, 1.0, 0.0).astype(BF16)
        w_rows = jnp.sum(jnp.where(hit, w_row, 0.0), axis=-1, keepdims=True)
        xg = _dot(pick, u_ref[rs, :]).astype(BF16)
        _swiglu_act(xg, wgu_ref.at[0], act, d_ff, chunk)
        y = (_dot(act[...], wd_ref[0]) * w_rows).astype(BF16)
        o_ref[rs, :] += _dot_tn(pick, y)

    big = sizes[-1]
    for s in range(n_sub):
        rs = slice(s * sub_t, (s + 1) * sub_t)
        args = (rs, rank_t_ref[s, pl.ds(e, 1), :], wts_t_ref[s, pl.ds(e, 1), :])
        count = cnt_ref[s * N_EXPERTS + e]
        n_full = count // big
        rem = count - n_full * big

        def full_block(b, carry, args=args):
            run_block(b * big, big, *args)
            return carry

        lax.fori_loop(0, n_full + (rem > sizes[-2]).astype(jnp.int32), full_block, 0)
        below = 0
        for size in sizes[:-1]:
            @pl.when(jnp.logical_and(rem > below, rem <= size))
            def _(size=size, args=args, n_full=n_full):
                run_block(n_full * big, size, *args)
            below = size


def moe(h, g, router, w_gu, w_down, *, tm=1024, sub_t=512, chunk=512, sizes=(128, 160, 192)):
    n, d = h.shape
    n_exp, d_ff, _ = w_down.shape
    n_sub = tm // sub_t
    tri = jnp.tril(jnp.ones((sub_t, sub_t), BF16), -1)
    r2 = _split_weight(router)
    return pl.pallas_call(
        functools.partial(_moe_kernel, chunk=chunk, sizes=sizes, sub_t=sub_t),
        grid=(n // tm, n_exp),
        in_specs=[pl.BlockSpec((tm, d), lambda i, e: (i, 0)), _resident((1, d)), _resident(r2.shape),
                  _resident(tri.shape),
                  pl.BlockSpec((1, d, 2 * d_ff), lambda i, e: (e, 0, 0)),
                  pl.BlockSpec((1, d_ff, d), lambda i, e: (e, 0, 0))],
        out_specs=pl.BlockSpec((tm, d), lambda i, e: (i, 0)),
        out_shape=jax.ShapeDtypeStruct((n, d), F32),
        scratch_shapes=[pltpu.VMEM((tm, d), BF16), pltpu.VMEM((n_sub, LANES, sub_t), F32),
                        pltpu.VMEM((n_sub, LANES, sub_t), F32), pltpu.VMEM((sizes[-1], d_ff), BF16),
                        pltpu.SMEM((n_sub * N_EXPERTS,), jnp.int32)],
        compiler_params=_params("parallel", "arbitrary"),
        name="moe",
    )(h, g, r2, tri, w_gu, w_down)


def _pair_mask(ii, jj, s):
    bi = ii // s
    return (bi % 2 == 1) & (jj // s == bi - 1)


def _bmm(a, b):
    return lax.dot_general(a, b, (((2,), (1,)), ((0,), (0,))), preferred_element_type=F32)


def _bmm_nt(a, b):
    return lax.dot_general(a, b, (((2,), (2,)), ((0,), (0,))), preferred_element_type=F32)


def _delta_kernel(q_ref, k_ref, v_ref, gate_ref, ba_ref, cq_ref, ck_ref, cv_ref, og_ref,
                  o_ref, p_s, n_s, egl_s, sall_s, xpad_s, *, heads):
    seq = q_ref.shape[1]
    ck = C_CHUNK
    nc = seq // ck
    dh = C_HEAD_DIM
    group = pl.program_id(1)
    lane = lax.broadcasted_iota(jnp.int32, (1, LANES), 1)
    ii = lax.broadcasted_iota(jnp.int32, (1, ck, ck), 1)
    jj = lax.broadcasted_iota(jnp.int32, (1, ck, ck), 2)
    eye = (ii == jj).astype(F32)

    gates = ba_ref[0]

    pad = xpad_s.shape[0] - seq
    xpad_s[0:pad, :] = jnp.zeros((pad, dh), F32)

    def conv_silu(x, w_ref, ls):
        kw = w_ref.shape[0]
        xpad_s[pad:, :] = x
        y = x * w_ref[kw - 1:kw, ls]
        for sh in range(1, kw):
            y = y + xpad_s[pad - sh:pad - sh + seq, :] * w_ref[kw - 1 - sh:kw - sh, ls]
        return _silu(y)

    def l2n(x):
        return x * lax.rsqrt(jnp.sum(x * x, axis=-1, keepdims=True) + EPS)

    prepared = []
    for g in range(heads):
        ls = slice(g * dh, (g + 1) * dh)
        head = group * heads + g
        q = l2n(conv_silu(q_ref[0, :, ls].astype(F32), cq_ref, ls)) * (dh ** -0.5)
        k = l2n(conv_silu(k_ref[0, :, ls].astype(F32), ck_ref, ls))
        v = conv_silu(v_ref[0, :, ls].astype(F32), cv_ref, ls)
        beta = jnp.broadcast_to(jnp.sum(jnp.where(lane == head, gates, 0.0), axis=-1, keepdims=True),
                                (seq, LANES))
        gcs = jnp.broadcast_to(jnp.sum(jnp.where(lane == head + C_HEADS, gates, 0.0), axis=-1, keepdims=True),
                               (seq, LANES))
        q3, k3, v3 = (t.reshape(nc, ck, dh) for t in (q, k, v))
        beta3 = beta.reshape(nc, ck, LANES)
        gcs3 = gcs.reshape(nc, ck, LANES)
        egc3 = jnp.exp(gcs3)
        glast = gcs3[:, ck - 1:ck, :]
        gc_row = jnp.swapaxes(gcs3, 1, 2)[:, :ck, :]
        decay = jnp.exp(jnp.where(ii >= jj, gcs3[:, :, :ck] - gc_row, NEG))
        kb = k3.astype(BF16)
        qk = _bmm_nt(jnp.concatenate([q3.astype(BF16), kb], axis=1), kb)
        attn = (qk[:, :ck] * decay).astype(BF16)
        a_mat = jnp.where(ii > jj, beta3[:, :, :ck] * qk[:, ck:] * decay, 0.0)
        t = eye - jnp.where(_pair_mask(ii, jj, 1), a_mat, 0.0)
        s = 2
        while s < ck:
            tb = t.astype(BF16)
            x = _bmm(tb, jnp.where(_pair_mask(ii, jj, s), a_mat, 0.0).astype(BF16))
            t = t - _bmm(x.astype(BF16), tb)
            s *= 2
        rhs = jnp.concatenate([v3 * beta3, k3 * (beta3 * egc3)], axis=2).astype(BF16)
        uw = _bmm(t.astype(BF16), rhs)
        kd_t = jnp.swapaxes(k3 * jnp.exp(glast - gcs3), 1, 2).astype(BF16)
        pn = _bmm(kd_t, uw.astype(BF16))
        n_s[g] = pn[:, :, :dh]
        p_s[g] = pn[:, :, dh:].astype(BF16)
        egl_s[g] = jnp.exp(glast)
        wq = jnp.concatenate([uw[:, :, dh:], q3 * egc3], axis=1).astype(BF16)
        prepared.append((wq, attn, uw[:, :, :dh]))

    def scan(c, states):
        new = []
        for g in range(heads):
            sb = states[g].astype(BF16)
            sall_s[g, c] = sb
            new.append(states[g] * egl_s[g, c] + n_s[g, c] - _dot(p_s[g, c], sb))
        return tuple(new)

    lax.fori_loop(0, nc, scan, tuple(jnp.zeros((dh, dh), F32) for _ in range(heads)))

    for g in range(heads):
        ls = slice(g * dh, (g + 1) * dh)
        wq, attn, u = prepared[g]
        r = _bmm(wq, sall_s[g])
        v_new = u - r[:, :ck]
        o = (r[:, ck:] + _bmm(attn, v_new.astype(BF16))).reshape(seq, dh)
        gate = gate_ref[0, :, ls].astype(F32)
        o_ref[0, :, ls] = (_rms_rows(o, og_ref[...]) * _silu(gate)).astype(o_ref.dtype)


def delta_mix(proj, ba, conv_w, o_g, *, batch, seq, heads=2):
    nh = C_HEADS
    ngroup = nh // heads
    width = heads * C_HEAD_DIM
    nc = seq // C_CHUNK

    def slab(off):
        return pl.BlockSpec((1, seq, width), lambda b, h: (b, 0, off * ngroup + h))

    def conv_slab(off):
        return pl.BlockSpec((conv_w.shape[0], width), lambda b, h: (0, off * ngroup + h))

    return pl.pallas_call(
        functools.partial(_delta_kernel, heads=heads),
        grid=(batch, ngroup),
        in_specs=[slab(0), slab(1), slab(2), slab(3),
                  pl.BlockSpec((1, seq, LANES), lambda b, h: (b, 0, 0)),
                  conv_slab(0), conv_slab(1), conv_slab(2), _resident((1, LANES))],
        out_specs=pl.BlockSpec((1, seq, width), lambda b, h: (b, 0, h)),
        out_shape=jax.ShapeDtypeStruct((batch, seq, nh * C_HEAD_DIM), BF16),
        scratch_shapes=[pltpu.VMEM((heads, nc, C_HEAD_DIM, C_HEAD_DIM), BF16),
                        pltpu.VMEM((heads, nc, C_HEAD_DIM, C_HEAD_DIM), F32),
                        pltpu.VMEM((heads, nc, 1, LANES), F32),
                        pltpu.VMEM((heads, nc, C_HEAD_DIM, C_HEAD_DIM), BF16),
                        pltpu.VMEM((seq + SUBLANES, C_HEAD_DIM), F32)],
        compiler_params=_params("parallel", "parallel"),
        name="delta_mix",
    )(proj, proj, proj, proj, ba, conv_w, conv_w, conv_w, o_g.reshape(1, LANES))


def kernel(x, mem, norm_mix, norm_xattn, norm_mem, norm_ffn, ev_w_in, ev_conv, ev_q_norm, ev_k_norm, ev_w_out, od_w_in, od_conv, od_a_log, od_dt_bias, od_o_norm, od_w_out, xa_w_q, xa_w_kv, xa_q_norm, xa_k_norm, xa_w_o, ff_w_gu, ff_w_down, moe_router, moe_w_gu, moe_w_down):
    batch, seq, d = x.shape
    n = batch * seq
    h = x.reshape(n, d)
    depth = norm_mix.shape[0]
    for layer in range(depth):
        i = layer // 2
        g_mix = norm_mix[layer].reshape(1, d)
        g_ffn = norm_ffn[layer].reshape(1, d)
        kv = mem_kv(mem, norm_mem[layer].reshape(1, d), xa_w_kv[layer].astype(BF16), xa_k_norm[layer])
        xattn_args = (norm_xattn[layer].reshape(1, d), xa_w_q[layer].astype(BF16), xa_q_norm[layer], kv,
                      xa_w_o[layer].astype(BF16))
        if layer % 2 == 0:
            proj = norm_proj(h, g_mix, ev_w_in[i].astype(BF16))
            a, o = ev_mix(proj.reshape(batch, seq, -1), ev_conv[i], ev_q_norm[i], ev_k_norm[i],
                          batch=batch, seq=seq)
            w_out = ev_w_out[i].astype(BF16)
            h = layer_tail(h, [a.reshape(n, -1), o.reshape(n, -1)], [w_out[:A_WIDTH], w_out[A_WIDTH:]],
                           *xattn_args, (g_ffn, ff_w_gu[i].astype(BF16), ff_w_down[i].astype(BF16)), seq=seq)
        else:
            n_main = 4 * C_HEADS * C_HEAD_DIM
            w_gates = _split_weight(od_w_in[i][:, n_main:])
            proj, ba = norm_proj(h, g_mix, od_w_in[i][:, :n_main].astype(BF16), w_gates, od_a_log[i],
                                 od_dt_bias[i])
            o = delta_mix(proj.reshape(batch, seq, -1), ba.reshape(batch, seq, -1), od_conv[i], od_o_norm[i],
                          batch=batch, seq=seq)
            h = layer_tail(h, [o.reshape(n, -1)], [od_w_out[i].astype(BF16)], *xattn_args, seq=seq)
            h = moe(h, g_ffn, moe_router[i], moe_w_gu[i].astype(BF16), moe_w_down[i].astype(BF16))
    return h.reshape(batch, seq, d)
```

```python
import functools

import jax
import jax.numpy as jnp
from jax import lax
from jax.experimental import pallas as pl
from jax.experimental.pallas import tpu as pltpu

F32 = jnp.float32
BF16 = jnp.bfloat16

EPS = 1e-6
A_WIDTH = 512
B_HEAD_DIM = 64
DILATED_PATTERN = ((128, 1), (512, 4), (2048, 16))
C_HEADS = 8
C_HEAD_DIM = 128
C_CHUNK = 64
X_HEADS = 4
X_HEAD_DIM = 64
N_EXPERTS = 8

LANES = 128
SUBLANES = 8
VMEM_LIMIT = 56 * 1024 * 1024
NEG = -1e30
LOG2_E = 1.4426950408889634


def _params(*sem):
    return pltpu.CompilerParams(dimension_semantics=sem, vmem_limit_bytes=VMEM_LIMIT)


def _resident(shape):
    nd = len(shape)
    return pl.BlockSpec(shape, lambda *_: (0,) * nd, pipeline_mode=pl.Buffered(1))


def _rms_rows(x, g):
    ms = jnp.mean(x * x, axis=-1, keepdims=True)
    return x * lax.rsqrt(ms + EPS) * g


def _silu(x):
    return x * jax.nn.sigmoid(x)


def _dot(a, b):
    return jnp.dot(a, b, preferred_element_type=F32)


def _dot_nt(a, b):
    return lax.dot_general(a, b, (((1,), (1,)), ((), ())), preferred_element_type=F32)


def _dot_tn(a, b):
    return lax.dot_general(a, b, (((0,), (0,)), ((), ())), preferred_element_type=F32)


def _head_inv_rms(x, head_dim):
    width = x.shape[-1]
    lane = lax.broadcasted_iota(jnp.int32, (1, width), 1)
    x2 = x * x
    inv = jnp.zeros_like(x)
    for h in range(width // head_dim):
        sel = (lane >= h * head_dim) & (lane < (h + 1) * head_dim)
        ms = jnp.sum(jnp.where(sel, x2, 0.0), axis=-1, keepdims=True) * (1.0 / head_dim)
        inv = jnp.where(sel, lax.rsqrt(ms + EPS), inv)
    return inv


def _norm_proj_kernel(x_ref, g_ref, w_ref, o_ref, *, chunk):
    xn = _rms_rows(x_ref[...], g_ref[...]).astype(BF16)
    for c in range(w_ref.shape[1] // chunk):
        cs = slice(c * chunk, (c + 1) * chunk)
        o_ref[:, cs] = _dot(xn, w_ref[:, cs]).astype(o_ref.dtype)


def _norm_proj_gates_kernel(x_ref, g_ref, w_ref, wg_ref, alog_ref, dtb_ref, o_ref, og_ref, *, chunk):
    xf = _rms_rows(x_ref[...], g_ref[...])
    xn = xf.astype(BF16)
    for c in range(w_ref.shape[1] // chunk):
        cs = slice(c * chunk, (c + 1) * chunk)
        o_ref[:, cs] = _dot(xn, w_ref[:, cs]).astype(o_ref.dtype)
    ba = _dot_split(xf, wg_ref)
    lane = lax.broadcasted_iota(jnp.int32, (1, LANES), 1)
    z = ba + dtb_ref[...]
    softplus = jnp.maximum(z, 0.0) + jnp.log(1.0 + jnp.exp(-jnp.abs(z)))
    cum = -jnp.exp(alog_ref[...]) * softplus
    in_chunk = lax.broadcasted_iota(jnp.int32, (ba.shape[0], 1), 0) % C_CHUNK
    sh = 1
    while sh < C_CHUNK:
        cum = cum + jnp.where(in_chunk >= sh, pltpu.roll(cum, sh, axis=0), 0.0)
        sh *= 2
    og_ref[...] = jnp.where(lane < C_HEADS, jax.nn.sigmoid(ba), cum)


def norm_proj(x, g, w, w_gates=None, a_log=None, dt_bias=None, *, tm=1024, chunk=512):
    n, d = x.shape
    nout = w.shape[1]
    in_specs = [pl.BlockSpec((tm, d), lambda i: (i, 0)), _resident((1, d)), _resident((d, nout))]
    if w_gates is None:
        return pl.pallas_call(
            functools.partial(_norm_proj_kernel, chunk=chunk),
            grid=(n // tm,),
            in_specs=in_specs,
            out_specs=pl.BlockSpec((tm, nout), lambda i: (i, 0)),
            out_shape=jax.ShapeDtypeStruct((n, nout), BF16),
            compiler_params=_params("parallel"),
            name="norm_proj",
        )(x, g, w)
    ng = w_gates.shape[1] // 2
    nh = a_log.shape[0]
    alog = jnp.pad(a_log.reshape(1, nh), ((0, 0), (nh, ng - 2 * nh)))
    dtb = jnp.pad(dt_bias.reshape(1, nh), ((0, 0), (nh, ng - 2 * nh)))
    return pl.pallas_call(
        functools.partial(_norm_proj_gates_kernel, chunk=chunk),
        grid=(n // tm,),
        in_specs=in_specs + [_resident(w_gates.shape), _resident((1, ng)), _resident((1, ng))],
        out_specs=[pl.BlockSpec((tm, nout), lambda i: (i, 0)), pl.BlockSpec((tm, ng), lambda i: (i, 0))],
        out_shape=[jax.ShapeDtypeStruct((n, nout), BF16), jax.ShapeDtypeStruct((n, ng), F32)],
        compiler_params=_params("parallel"),
        name="norm_proj_gates",
    )(x, g, w, w_gates, alog, dtb)


def _dilation_bias(seq, blk):
    dist = (seq - blk) + jnp.arange(blk)[:, None] - jnp.arange(seq)[None, :]
    mult = jnp.zeros(dist.shape, F32)
    for window, d in DILATED_PATTERN:
        mult = mult + ((dist >= 0) & (dist % d == 0) & (dist <= window)).astype(F32)
    return jnp.where(mult > 0, jnp.log2(jnp.maximum(mult, 1.0)), NEG)


def _ev_mix_kernel(gb_ref, gc_ref, xc_ref, q_ref, k_ref, v_ref, cw_ref, qg_ref, kg_ref, bias_ref,
                   a_ref, o_ref, kn_ref, qh_ref, vh_ref, s_ref, *, blk, kchunk):
    seq = q_ref.shape[1]
    y = gc_ref[0].astype(F32) * xc_ref[0].astype(F32)
    row = lax.broadcasted_iota(jnp.int32, (seq, 1), 0)
    kw = cw_ref.shape[0]
    conv = y * cw_ref[kw - 1:kw, :]
    for sh in range(1, kw):
        ys = jnp.where(row >= sh, pltpu.roll(y, sh, axis=0), 0.0)
        conv = conv + ys * cw_ref[kw - 1 - sh:kw - sh, :]
    a_ref[0] = (gb_ref[0].astype(F32) * conv).astype(a_ref.dtype)

    lane = lax.broadcasted_iota(jnp.int32, (1, LANES), 1)
    first = lane < B_HEAD_DIM
    kf = k_ref[0].astype(F32)
    kn_ref[...] = (kf * _head_inv_rms(kf, B_HEAD_DIM) * kg_ref[...]).astype(BF16)
    scale = B_HEAD_DIM ** -0.5 * LOG2_E
    qf = q_ref[0].astype(F32)
    qn = qf * _head_inv_rms(qf, B_HEAD_DIM) * (qg_ref[...] * scale)
    vb = v_ref[0]
    one = jnp.ones((), BF16)
    for h in range(2):
        own = first if h == 0 else jnp.logical_not(first)
        qh_ref[h] = jnp.where(own, qn, 0.0).astype(BF16)
        vh_ref[h] = jnp.where(own, vb, one)

    def lane_groups(t):
        return [t[:, c:c + LANES] for c in range(0, t.shape[1], LANES)]

    def chunks_of(qi):
        kv_len = (qi + 1) * blk
        return [(lo, min(lo + kchunk, kv_len)) for lo in range(0, kv_len, kchunk)]

    def score_pass(qi):
        r0 = qi * blk
        ms = []
        for h in range(2):
            qh = qh_ref[h, r0:r0 + blk, :]
            mv = None
            for lo, hi in chunks_of(qi):
                c0 = lo + seq - blk - r0
                s = _dot_nt(qh, kn_ref[lo:hi, :]) + bias_ref[:, c0:c0 + hi - lo]
                s_ref[qi % 2, h, :, lo:hi] = s
                for part in lane_groups(s.astype(BF16)):
                    mv = part if mv is None else jnp.maximum(mv, part)
            ms.append(jnp.max(mv.astype(F32), axis=-1, keepdims=True))
        return ms

    def value_pass(qi, ms):
        r0 = qi * blk
        outs = []
        for h in range(2):
            acc = jnp.zeros((blk, LANES), F32)
            for lo, hi in chunks_of(qi):
                p = jnp.exp2(s_ref[qi % 2, h, :, lo:hi] - ms[h])
                acc = acc + _dot(p.astype(BF16), vh_ref[h, lo:hi, :])
            other = jnp.logical_not(first) if h == 0 else first
            l = jnp.sum(jnp.where(other, acc, 0.0), axis=-1, keepdims=True) * (1.0 / B_HEAD_DIM)
            outs.append(acc / l)
        o_ref[0, r0:r0 + blk, :] = jnp.where(first, outs[0], outs[1]).astype(o_ref.dtype)

    nq = seq // blk
    pending = score_pass(0)
    for qi in range(1, nq):
        ahead = score_pass(qi)
        value_pass(qi - 1, pending)
        pending = ahead
    value_pass(nq - 1, pending)


def ev_mix(proj, conv_w, q_g, k_g, *, batch, seq, blk=256, kchunk=256):
    nslab = A_WIDTH // LANES
    bias = _dilation_bias(seq, blk)
    qg2 = jnp.tile(q_g.reshape(1, -1), (1, LANES // B_HEAD_DIM))
    kg2 = jnp.tile(k_g.reshape(1, -1), (1, LANES // B_HEAD_DIM))

    def slab(off):
        return pl.BlockSpec((1, seq, LANES), lambda b, s: (b, 0, off * nslab + s))

    out_spec = pl.BlockSpec((1, seq, LANES), lambda b, s: (b, 0, s))
    return pl.pallas_call(
        functools.partial(_ev_mix_kernel, blk=blk, kchunk=kchunk),
        grid=(batch, nslab),
        in_specs=[slab(0), slab(1), slab(2), slab(3), slab(4), slab(5),
                  pl.BlockSpec((conv_w.shape[0], LANES), lambda b, s: (0, s)),
                  _resident((1, LANES)), _resident((1, LANES)), _resident(bias.shape)],
        out_specs=[out_spec, out_spec],
        out_shape=[jax.ShapeDtypeStruct((batch, seq, A_WIDTH), BF16)] * 2,
        scratch_shapes=[pltpu.VMEM((seq, LANES), BF16), pltpu.VMEM((2, seq, LANES), BF16),
                        pltpu.VMEM((2, seq, LANES), BF16), pltpu.VMEM((2, 2, blk, seq), F32)],
        compiler_params=_params("parallel", "parallel"),
        name="ev_mix",
    )(proj, proj, proj, proj, proj, proj, conv_w, qg2, kg2, bias)


def _mem_kv_kernel(m_ref, g_ref, w_ref, kg_ref, o_ref):
    width = X_HEADS * X_HEAD_DIM
    mn = _rms_rows(m_ref[0], g_ref[...]).astype(BF16)
    kv = _dot(mn, w_ref[...])
    k = kv[:, :width]
    o_ref[0, :, :width] = (k * _head_inv_rms(k, X_HEAD_DIM) * kg_ref[...]).astype(o_ref.dtype)
    o_ref[0, :, width:] = kv[:, width:].astype(o_ref.dtype)


def mem_kv(mem, g, w_kv, k_g):
    b, m, d = mem.shape
    width = w_kv.shape[1]
    kg = jnp.tile(k_g.reshape(1, -1), (1, X_HEADS))
    return pl.pallas_call(
        _mem_kv_kernel,
        grid=(b,),
        in_specs=[pl.BlockSpec((1, m, d), lambda i: (i, 0, 0)), _resident((1, d)), _resident(w_kv.shape),
                  _resident(kg.shape)],
        out_specs=pl.BlockSpec((1, m, width), lambda i: (i, 0, 0)),
        out_shape=jax.ShapeDtypeStruct((b, m, width), BF16),
        compiler_params=_params("parallel"),
        name="mem_kv",
    )(mem, g, w_kv, kg)


def _xattn_update(x, g_ref, wq_ref, qg_ref, kv_ref, wo_ref):
    width = X_HEADS * X_HEAD_DIM
    u = _rms_rows(x, g_ref[...]).astype(BF16)
    q = _dot(u, wq_ref[...])
    q = q * _head_inv_rms(q, X_HEAD_DIM) * (qg_ref[...] * X_HEAD_DIM ** -0.5)
    k = kv_ref[0, :, :width]
    v = kv_ref[0, :, width:]
    lane = lax.broadcasted_iota(jnp.int32, (1, width), 1)
    out = jnp.zeros(q.shape, F32)
    for h in range(X_HEADS):
        sel = (lane >= h * X_HEAD_DIM) & (lane < (h + 1) * X_HEAD_DIM)
        s = _dot_nt(jnp.where(sel, q, 0.0).astype(BF16), k)
        p = jnp.exp(s - jnp.max(s, axis=-1, keepdims=True))
        oh = _dot(p.astype(BF16), v) / jnp.sum(p, axis=-1, keepdims=True)
        out = jnp.where(sel, oh, out)
    return x + _dot(out.astype(BF16), wo_ref[...])


def _swiglu_act(u, wgu_ref, act_ref, d_ff, chunk):
    for lo in range(0, d_ff, chunk):
        hi = min(lo + chunk, d_ff)
        gate = _dot(u, wgu_ref[:, lo:hi])
        up = _dot(u, wgu_ref[:, d_ff + lo:d_ff + hi])
        act_ref[:, lo:hi] = (_silu(gate) * up).astype(act_ref.dtype)


def _tail_kernel(*refs, n_in, with_ffn, chunk):
    h_ref = refs[0]
    xs = refs[1:1 + n_in]
    ws = refs[1 + n_in:1 + 2 * n_in]
    xattn_refs = refs[1 + 2 * n_in:6 + 2 * n_in]
    x = h_ref[...]
    for x_ref, w_ref in zip(xs, ws):
        x = x + _dot(x_ref[...], w_ref[...])
    x = _xattn_update(x, *xattn_refs)
    if with_ffn:
        gf_ref, wgu_ref, wd_ref, o_ref, act_ref = refs[6 + 2 * n_in:]
        u = _rms_rows(x, gf_ref[...]).astype(BF16)
        _swiglu_act(u, wgu_ref, act_ref, wd_ref.shape[0], chunk)
        x = x + _dot(act_ref[...], wd_ref[...])
    else:
        o_ref = refs[-1]
    o_ref[...] = x


def layer_tail(h, xs, ws, g_x, w_q, q_g, kv, w_o, ffn_weights=None, *, seq, tm=512, chunk=256):
    n, d = h.shape
    qg = jnp.tile(q_g.reshape(1, -1), (1, X_HEADS))
    per_seq = seq // tm
    row = lambda i: (i, 0)
    in_specs = [pl.BlockSpec((tm, d), row)]
    in_specs += [pl.BlockSpec((tm, x.shape[1]), row) for x in xs]
    in_specs += [_resident(w.shape) for w in ws]
    in_specs += [_resident((1, d)), _resident(w_q.shape), _resident(qg.shape),
                 pl.BlockSpec((1,) + kv.shape[1:], lambda i: (i // per_seq, 0, 0)), _resident(w_o.shape)]
    args = [h, *xs, *ws, g_x, w_q, qg, kv, w_o]
    scratch = []
    if ffn_weights is not None:
        g_f, w_gu, w_down = ffn_weights
        in_specs += [_resident((1, d)), _resident(w_gu.shape), _resident(w_down.shape)]
        args += [g_f, w_gu, w_down]
        scratch = [pltpu.VMEM((tm, w_down.shape[0]), BF16)]
    return pl.pallas_call(
        functools.partial(_tail_kernel, n_in=len(xs), with_ffn=ffn_weights is not None, chunk=chunk),
        grid=(n // tm,),
        in_specs=in_specs,
        out_specs=pl.BlockSpec((tm, d), row),
        out_shape=jax.ShapeDtypeStruct((n, d), F32),
        scratch_shapes=scratch,
        compiler_params=_params("parallel"),
        name="layer_tail",
    )(*args)


def _split_bf16(x):
    hi = x.astype(BF16)
    return hi, (x - hi.astype(F32)).astype(BF16)


def _dot_split(x, w2_ref):
    half = w2_ref.shape[1] // 2
    xh, xl = _split_bf16(x)
    both = _dot(xh, w2_ref[...])
    return both[:, :half] + both[:, half:] + _dot(xl, w2_ref[:, :half])


def _split_weight(w):
    w = jnp.pad(w, ((0, 0), (0, LANES - w.shape[1])))
    hi = w.astype(BF16)
    return jnp.concatenate([hi, (w - hi.astype(F32)).astype(BF16)], axis=1)


def _moe_kernel(h_ref, g_ref, r2_ref, tri_ref, wgu_ref, wd_ref, o_ref,
                u_ref, rank_t_ref, wts_t_ref, act_ref, cnt_ref, *, chunk, sizes, sub_t):
    e = pl.program_id(1)
    n_sub = h_ref.shape[0] // sub_t
    d_ff = wd_ref.shape[1]
    lane = lax.broadcasted_iota(jnp.int32, (1, LANES), 1)

    @pl.when(e == 0)
    def _():
        x = h_ref[...]
        uf = _rms_rows(x, g_ref[...])
        u_ref[...] = uf.astype(BF16)
        logits = _dot_split(uf, r2_ref)
        logits = jnp.where(lane < N_EXPERTS, logits, NEG)
        m1 = jnp.max(logits, axis=-1, keepdims=True)
        i1 = jnp.min(jnp.where(logits == m1, lane, LANES), axis=-1, keepdims=True)
        rest = jnp.where(lane == i1, NEG, logits)
        m2 = jnp.max(rest, axis=-1, keepdims=True)
        i2 = jnp.min(jnp.where(rest == m2, lane, LANES), axis=-1, keepdims=True)
        e2 = jnp.exp(m2 - m1)
        w1 = 1.0 / (1.0 + e2)
        wts = jnp.where(lane == i1, w1, 0.0) + jnp.where(lane == i2, e2 * w1, 0.0)
        sel = jnp.where(lane == i1, 1.0, jnp.where(lane == i2, 1.0, 0.0))
        for s in range(n_sub):
            rs = slice(s * sub_t, (s + 1) * sub_t)
            sel_s = sel[rs]
            rank = jnp.where(sel_s > 0.0, _dot(tri_ref[...], sel_s.astype(BF16)), -1.0)
            rank_t_ref[s] = jnp.transpose(rank)
            wts_t_ref[s] = jnp.transpose(wts[rs])
            counts = jnp.sum(sel_s, axis=0, keepdims=True)
            for ee in range(N_EXPERTS):
                cnt_ref[s * N_EXPERTS + ee] = jnp.sum(jnp.where(lane == ee, counts, 0.0)).astype(jnp.int32)
        o_ref[...] = x

    def run_block(first, size, rs, rank_row, w_row):
        sub = lax.broadcasted_iota(jnp.int32, (size, 1), 0).astype(F32)
        act = act_ref.at[pl.ds(0, size)]
        hit = rank_row == first.astype(F32) + sub
        pick = jnp.where(hit, 1.0, 0.0).astype(BF16)
        w_rows = jnp.sum(jnp.where(hit, w_row, 0.0), axis=-1, keepdims=True)
        xg = _dot(pick, u_ref[rs, :]).astype(BF16)
        _swiglu_act(xg, wgu_ref.at[0], act, d_ff, chunk)
        y = (_dot(act[...], wd_ref[0]) * w_rows).astype(BF16)
        o_ref[rs, :] += _dot_tn(pick, y)

    big = sizes[-1]
    for s in range(n_sub):
        rs = slice(s * sub_t, (s + 1) * sub_t)
        args = (rs, rank_t_ref[s, pl.ds(e, 1), :], wts_t_ref[s, pl.ds(e, 1), :])
        count = cnt_ref[s * N_EXPERTS + e]
        n_full = count // big
        rem = count - n_full * big

        def full_block(b, carry, args=args):
            run_block(b * big, big, *args)
            return carry

        lax.fori_loop(0, n_full + (rem > sizes[-2]).astype(jnp.int32), full_block, 0)
        below = 0
        for size in sizes[:-1]:
            @pl.when(jnp.logical_and(rem > below, rem <= size))
            def _(size=size, args=args, n_full=n_full):
                run_block(n_full * big, size, *args)
            below = size


def moe(h, g, router, w_gu, w_down, *, tm=1024, sub_t=512, chunk=512, sizes=(128, 160, 192)):
    n, d = h.shape
    n_exp, d_ff, _ = w_down.shape
    n_sub = tm // sub_t
    tri = jnp.tril(jnp.ones((sub_t, sub_t), BF16), -1)
    r2 = _split_weight(router)
    return pl.pallas_call(
        functools.partial(_moe_kernel, chunk=chunk, sizes=sizes, sub_t=sub_t),
        grid=(n // tm, n_exp),
        in_specs=[pl.BlockSpec((tm, d), lambda i, e: (i, 0)), _resident((1, d)), _resident(r2.shape),
                  _resident(tri.shape),
                  pl.BlockSpec((1, d, 2 * d_ff), lambda i, e: (e, 0, 0)),
                  pl.BlockSpec((1, d_ff, d), lambda i, e: (e, 0, 0))],
        out_specs=pl.BlockSpec((tm, d), lambda i, e: (i, 0)),
        out_shape=jax.ShapeDtypeStruct((n, d), F32),
        scratch_shapes=[pltpu.VMEM((tm, d), BF16), pltpu.VMEM((n_sub, LANES, sub_t), F32),
                        pltpu.VMEM((n_sub, LANES, sub_t), F32), pltpu.VMEM((sizes[-1], d_ff), BF16),
                        pltpu.SMEM((n_sub * N_EXPERTS,), jnp.int32)],
        compiler_params=_params("parallel", "arbitrary"),
        name="moe",
    )(h, g, r2, tri, w_gu, w_down)


def _pair_mask(ii, jj, s):
    bi = ii // s
    return (bi % 2 == 1) & (jj // s == bi - 1)


def _bmm(a, b):
    return lax.dot_general(a, b, (((2,), (1,)), ((0,), (0,))), preferred_element_type=F32)


def _bmm_nt(a, b):
    return lax.dot_general(a, b, (((2,), (2,)), ((0,), (0,))), preferred_element_type=F32)


def _delta_kernel(q_ref, k_ref, v_ref, gate_ref, ba_ref, cq_ref, ck_ref, cv_ref, og_ref,
                  o_ref, p_s, n_s, egl_s, sall_s, xpad_s, *, heads):
    seq = q_ref.shape[1]
    ck = C_CHUNK
    nc = seq // ck
    dh = C_HEAD_DIM
    group = pl.program_id(1)
    lane = lax.broadcasted_iota(jnp.int32, (1, LANES), 1)
    ii = lax.broadcasted_iota(jnp.int32, (1, ck, ck), 1)
    jj = lax.broadcasted_iota(jnp.int32, (1, ck, ck), 2)
    eye = (ii == jj).astype(F32)

    gates = ba_ref[0]

    pad = xpad_s.shape[0] - seq
    xpad_s[0:pad, :] = jnp.zeros((pad, dh), F32)

    def conv_silu(x, w_ref, ls):
        kw = w_ref.shape[0]
        xpad_s[pad:, :] = x
        y = x * w_ref[kw - 1:kw, ls]
        for sh in range(1, kw):
            y = y + xpad_s[pad - sh:pad - sh + seq, :] * w_ref[kw - 1 - sh:kw - sh, ls]
        return _silu(y)

    def l2n(x):
        return x * lax.rsqrt(jnp.sum(x * x, axis=-1, keepdims=True) + EPS)

    prepared = []
    for g in range(heads):
        ls = slice(g * dh, (g + 1) * dh)
        head = group * heads + g
        q = l2n(conv_silu(q_ref[0, :, ls].astype(F32), cq_ref, ls)) * (dh ** -0.5)
        k = l2n(conv_silu(k_ref[0, :, ls].astype(F32), ck_ref, ls))
        v = conv_silu(v_ref[0, :, ls].astype(F32), cv_ref, ls)
        beta = jnp.broadcast_to(jnp.sum(jnp.where(lane == head, gates, 0.0), axis=-1, keepdims=True),
                                (seq, LANES))
        gcs = jnp.broadcast_to(jnp.sum(jnp.where(lane == head + C_HEADS, gates, 0.0), axis=-1, keepdims=True),
                               (seq, LANES))
        q3, k3, v3 = (t.reshape(nc, ck, dh) for t in (q, k, v))
        beta3 = beta.reshape(nc, ck, LANES)
        gcs3 = gcs.reshape(nc, ck, LANES)
        egc3 = jnp.exp(gcs3)
        glast = gcs3[:, ck - 1:ck, :]
        gc_row = jnp.swapaxes(gcs3, 1, 2)[:, :ck, :]
        decay = jnp.exp(jnp.where(ii >= jj, gcs3[:, :, :ck] - gc_row, NEG))
        kb = k3.astype(BF16)
        qk = _bmm_nt(jnp.concatenate([q3.astype(BF16), kb], axis=1), kb)
        attn = (qk[:, :ck] * decay).astype(BF16)
        a_mat = jnp.where(ii > jj, beta3[:, :, :ck] * qk[:, ck:] * decay, 0.0)
        t = eye - jnp.where(_pair_mask(ii, jj, 1), a_mat, 0.0)
        s = 2
        while s < ck:
            tb = t.astype(BF16)
            x = _bmm(tb, jnp.where(_pair_mask(ii, jj, s), a_mat, 0.0).astype(BF16))
            t = t - _bmm(x.astype(BF16), tb)
            s *= 2
        rhs = jnp.concatenate([v3 * beta3, k3 * (beta3 * egc3)], axis=2).astype(BF16)
        uw = _bmm(t.astype(BF16), rhs)
        kd_t = jnp.swapaxes(k3 * jnp.exp(glast - gcs3), 1, 2).astype(BF16)
        pn = _bmm(kd_t, uw.astype(BF16))
        n_s[g] = pn[:, :, :dh].astype(BF16)
        p_s[g] = pn[:, :, dh:].astype(BF16)
        egl_s[g] = jnp.exp(glast)
        wq = jnp.concatenate([uw[:, :, dh:], q3 * egc3], axis=1).astype(BF16)
        prepared.append((wq, attn, uw[:, :, :dh].astype(BF16)))

    def scan(c, states):
        new = []
        for g in range(heads):
            sb = states[g].astype(BF16)
            sall_s[g, c] = sb
            new.append(states[g] * egl_s[g, c] + n_s[g, c] - _dot(p_s[g, c], sb))
        return tuple(new)

    lax.fori_loop(0, nc, scan, tuple(jnp.zeros((dh, dh), F32) for _ in range(heads)))

    for g in range(heads):
        ls = slice(g * dh, (g + 1) * dh)
        wq, attn, u = prepared[g]
        r = _bmm(wq, sall_s[g])
        v_new = u.astype(F32) - r[:, :ck]
        o = (r[:, ck:] + _bmm(attn, v_new.astype(BF16))).reshape(seq, dh)
        gate = gate_ref[0, :, ls].astype(F32)
        o_ref[0, :, ls] = (_rms_rows(o, og_ref[...]) * _silu(gate)).astype(o_ref.dtype)


def delta_mix(proj, ba, conv_w, o_g, *, batch, seq, heads=4):
    nh = C_HEADS
    ngroup = nh // heads
    width = heads * C_HEAD_DIM
    nc = seq // C_CHUNK

    def slab(off):
        return pl.BlockSpec((1, seq, width), lambda b, h: (b, 0, off * ngroup + h))

    def conv_slab(off):
        return pl.BlockSpec((conv_w.shape[0], width), lambda b, h: (0, off * ngroup + h))

    return pl.pallas_call(
        functools.partial(_delta_kernel, heads=heads),
        grid=(batch, ngroup),
        in_specs=[slab(0), slab(1), slab(2),
                  pl.BlockSpec((1, seq, width), lambda b, h: (b, 0, 3 * ngroup + h), pipeline_mode=pl.Buffered(1)),
                  pl.BlockSpec((1, seq, LANES), lambda b, h: (b, 0, 0)),
                  conv_slab(0), conv_slab(1), conv_slab(2), _resident((1, LANES))],
        out_specs=pl.BlockSpec((1, seq, width), lambda b, h: (b, 0, h)),
        out_shape=jax.ShapeDtypeStruct((batch, seq, nh * C_HEAD_DIM), BF16),
        scratch_shapes=[pltpu.VMEM((heads, nc, C_HEAD_DIM, C_HEAD_DIM), BF16),
                        pltpu.VMEM((heads, nc, C_HEAD_DIM, C_HEAD_DIM), BF16),
                        pltpu.VMEM((heads, nc, 1, LANES), F32),
                        pltpu.VMEM((heads, nc, C_HEAD_DIM, C_HEAD_DIM), BF16),
                        pltpu.VMEM((seq + SUBLANES, C_HEAD_DIM), F32)],
        compiler_params=_params("parallel", "parallel"),
        name="delta_mix",
    )(proj, proj, proj, proj, ba, conv_w, conv_w, conv_w, o_g.reshape(1, LANES))


def kernel(x, mem, norm_mix, norm_xattn, norm_mem, norm_ffn, ev_w_in, ev_conv, ev_q_norm, ev_k_norm, ev_w_out, od_w_in, od_conv, od_a_log, od_dt_bias, od_o_norm, od_w_out, xa_w_q, xa_w_kv, xa_q_norm, xa_k_norm, xa_w_o, ff_w_gu, ff_w_down, moe_router, moe_w_gu, moe_w_down):
    batch, seq, d = x.shape
    n = batch * seq
    h = x.reshape(n, d)
    depth = norm_mix.shape[0]
    for layer in range(depth):
        i = layer // 2
        g_mix = norm_mix[layer].reshape(1, d)
        g_ffn = norm_ffn[layer].reshape(1, d)
        kv = mem_kv(mem, norm_mem[layer].reshape(1, d), xa_w_kv[layer].astype(BF16), xa_k_norm[layer])
        xattn_args = (norm_xattn[layer].reshape(1, d), xa_w_q[layer].astype(BF16), xa_q_norm[layer], kv,
                      xa_w_o[layer].astype(BF16))
        if layer % 2 == 0:
            proj = norm_proj(h, g_mix, ev_w_in[i].astype(BF16))
            a, o = ev_mix(proj.reshape(batch, seq, -1), ev_conv[i], ev_q_norm[i], ev_k_norm[i],
                          batch=batch, seq=seq)
            w_out = ev_w_out[i].astype(BF16)
            h = layer_tail(h, [a.reshape(n, -1), o.reshape(n, -1)], [w_out[:A_WIDTH], w_out[A_WIDTH:]],
                           *xattn_args, (g_ffn, ff_w_gu[i].astype(BF16), ff_w_down[i].astype(BF16)), seq=seq)
        else:
            n_main = 4 * C_HEADS * C_HEAD_DIM
            w_gates = _split_weight(od_w_in[i][:, n_main:])
            proj, ba = norm_proj(h, g_mix, od_w_in[i][:, :n_main].astype(BF16), w_gates, od_a_log[i],
                                 od_dt_bias[i])
            o = delta_mix(proj.reshape(batch, seq, -1), ba.reshape(batch, seq, -1), od_conv[i], od_o_norm[i],
                          batch=batch, seq=seq)
            h = layer_tail(h, [o.reshape(n, -1)], [od_w_out[i].astype(BF16)], *xattn_args, seq=seq)
            h = moe(h, g_ffn, moe_router[i], moe_w_gu[i].astype(BF16), moe_w_down[i].astype(BF16))
    return h.reshape(batch, seq, d)
```

```python
import functools

import jax
import jax.numpy as jnp
from jax import lax
from jax.experimental import pallas as pl
from jax.experimental.pallas import tpu as pltpu

F32 = jnp.float32
BF16 = jnp.bfloat16

EPS = 1e-6
D_MODEL = 1024
A_WIDTH = 512
B_HEADS = 8
B_HEAD_DIM = 64
DILATED_PATTERN = ((128, 1), (512, 4), (2048, 16))
C_HEADS = 8
C_HEAD_DIM = 128
C_CHUNK = 64
X_HEADS = 4
X_HEAD_DIM = 64
N_EXPERTS = 8

LANES = 128
SUBLANES = 8
VMEM_LIMIT = 56 * 1024 * 1024
NEG = -1e30
LOG2_E = 1.4426950408889634


def _params(*sem):
    return pltpu.CompilerParams(dimension_semantics=sem, vmem_limit_bytes=VMEM_LIMIT)


def _resident(shape):
    nd = len(shape)
    return pl.BlockSpec(shape, lambda *_: (0,) * nd, pipeline_mode=pl.Buffered(1))


def _rms_rows(x, g):
    ms = jnp.mean(x * x, axis=-1, keepdims=True)
    return x * lax.rsqrt(ms + EPS) * g


def _silu(x):
    return x * jax.nn.sigmoid(x)


def _dot(a, b):
    return jnp.dot(a, b, preferred_element_type=F32)


def _dot_nt(a, b):
    return lax.dot_general(a, b, (((1,), (1,)), ((), ())), preferred_element_type=F32)


def _dot_tn(a, b):
    return lax.dot_general(a, b, (((0,), (0,)), ((), ())), preferred_element_type=F32)


def _head_inv_rms(x, head_dim):
    width = x.shape[-1]
    lane = lax.broadcasted_iota(jnp.int32, (1, width), 1)
    x2 = x * x
    inv = jnp.zeros_like(x)
    for h in range(width // head_dim):
        sel = (lane >= h * head_dim) & (lane < (h + 1) * head_dim)
        ms = jnp.sum(jnp.where(sel, x2, 0.0), axis=-1, keepdims=True) * (1.0 / head_dim)
        inv = jnp.where(sel, lax.rsqrt(ms + EPS), inv)
    return inv


def _norm_proj_kernel(x_ref, g_ref, w_ref, o_ref, *, chunk):
    xn = _rms_rows(x_ref[...], g_ref[...]).astype(BF16)
    for c in range(w_ref.shape[1] // chunk):
        cs = slice(c * chunk, (c + 1) * chunk)
        o_ref[:, cs] = _dot(xn, w_ref[:, cs]).astype(o_ref.dtype)


def _norm_proj_gates_kernel(x_ref, g_ref, w_ref, wg_ref, alog_ref, dtb_ref, o_ref, og_ref, *, chunk):
    xf = _rms_rows(x_ref[...], g_ref[...])
    xn = xf.astype(BF16)
    for c in range(w_ref.shape[1] // chunk):
        cs = slice(c * chunk, (c + 1) * chunk)
        o_ref[:, cs] = _dot(xn, w_ref[:, cs]).astype(o_ref.dtype)
    ba = _dot_split(xf, wg_ref)
    lane = lax.broadcasted_iota(jnp.int32, (1, LANES), 1)
    z = ba + dtb_ref[...]
    softplus = jnp.maximum(z, 0.0) + jnp.log(1.0 + jnp.exp(-jnp.abs(z)))
    cum = -jnp.exp(alog_ref[...]) * softplus
    in_chunk = lax.broadcasted_iota(jnp.int32, (ba.shape[0], 1), 0) % C_CHUNK
    sh = 1
    while sh < C_CHUNK:
        cum = cum + jnp.where(in_chunk >= sh, pltpu.roll(cum, sh, axis=0), 0.0)
        sh *= 2
    og_ref[...] = jnp.where(lane < C_HEADS, jax.nn.sigmoid(ba), cum)


def norm_proj(x, g, w, w_gates=None, a_log=None, dt_bias=None, *, tm=1024, chunk=512):
    n, d = x.shape
    nout = w.shape[1]
    in_specs = [pl.BlockSpec((tm, d), lambda i: (i, 0)), _resident((1, d)), _resident((d, nout))]
    if w_gates is None:
        return pl.pallas_call(
            functools.partial(_norm_proj_kernel, chunk=chunk),
            grid=(n // tm,),
            in_specs=in_specs,
            out_specs=pl.BlockSpec((tm, nout), lambda i: (i, 0)),
            out_shape=jax.ShapeDtypeStruct((n, nout), BF16),
            compiler_params=_params("parallel"),
            name="norm_proj",
        )(x, g, w)
    ng = w_gates.shape[1] // 2
    nh = a_log.shape[0]
    alog = jnp.pad(a_log.reshape(1, nh), ((0, 0), (nh, ng - 2 * nh)))
    dtb = jnp.pad(dt_bias.reshape(1, nh), ((0, 0), (nh, ng - 2 * nh)))
    return pl.pallas_call(
        functools.partial(_norm_proj_gates_kernel, chunk=chunk),
        grid=(n // tm,),
        in_specs=in_specs + [_resident(w_gates.shape), _resident((1, ng)), _resident((1, ng))],
        out_specs=[pl.BlockSpec((tm, nout), lambda i: (i, 0)), pl.BlockSpec((tm, ng), lambda i: (i, 0))],
        out_shape=[jax.ShapeDtypeStruct((n, nout), BF16), jax.ShapeDtypeStruct((n, ng), F32)],
        compiler_params=_params("parallel"),
        name="norm_proj_gates",
    )(x, g, w, w_gates, alog, dtb)


def _dilation_bias(seq, blk):
    dist = (seq - blk) + jnp.arange(blk)[:, None] - jnp.arange(seq)[None, :]
    mult = jnp.zeros(dist.shape, F32)
    for window, d in DILATED_PATTERN:
        mult = mult + ((dist >= 0) & (dist % d == 0) & (dist <= window)).astype(F32)
    return jnp.where(mult > 0, jnp.log2(jnp.maximum(mult, 1.0)), NEG)


def _ev_mix_kernel(gb_ref, gc_ref, xc_ref, q_ref, k_ref, v_ref, cw_ref, qg_ref, kg_ref, bias_ref,
                   a_ref, o_ref, kn_ref, qh_ref, vh_ref, s_ref, *, blk, kchunk):
    seq = q_ref.shape[1]
    y = gc_ref[0].astype(F32) * xc_ref[0].astype(F32)
    row = lax.broadcasted_iota(jnp.int32, (seq, 1), 0)
    kw = cw_ref.shape[0]
    conv = y * cw_ref[kw - 1:kw, :]
    for sh in range(1, kw):
        ys = jnp.where(row >= sh, pltpu.roll(y, sh, axis=0), 0.0)
        conv = conv + ys * cw_ref[kw - 1 - sh:kw - sh, :]
    a_ref[0] = (gb_ref[0].astype(F32) * conv).astype(a_ref.dtype)

    lane = lax.broadcasted_iota(jnp.int32, (1, LANES), 1)
    first = lane < B_HEAD_DIM
    kf = k_ref[0].astype(F32)
    kn_ref[...] = (kf * _head_inv_rms(kf, B_HEAD_DIM) * kg_ref[...]).astype(BF16)
    scale = B_HEAD_DIM ** -0.5 * LOG2_E
    qf = q_ref[0].astype(F32)
    qn = qf * _head_inv_rms(qf, B_HEAD_DIM) * (qg_ref[...] * scale)
    vb = v_ref[0]
    one = jnp.ones((), BF16)
    for h in range(2):
        own = first if h == 0 else jnp.logical_not(first)
        qh_ref[h] = jnp.where(own, qn, 0.0).astype(BF16)
        vh_ref[h] = jnp.where(own, vb, one)

    def lane_groups(t):
        return [t[:, c:c + LANES] for c in range(0, t.shape[1], LANES)]

    def chunks_of(qi):
        kv_len = (qi + 1) * blk
        return [(lo, min(lo + kchunk, kv_len)) for lo in range(0, kv_len, kchunk)]

    def score_pass(qi):
        r0 = qi * blk
        ms = []
        for h in range(2):
            qh = qh_ref[h, r0:r0 + blk, :]
            mv = None
            for lo, hi in chunks_of(qi):
                c0 = lo + seq - blk - r0
                s = _dot_nt(qh, kn_ref[lo:hi, :]) + bias_ref[:, c0:c0 + hi - lo]
                s_ref[qi % 2, h, :, lo:hi] = s
                for part in lane_groups(s.astype(BF16)):
                    mv = part if mv is None else jnp.maximum(mv, part)
            ms.append(jnp.max(mv.astype(F32), axis=-1, keepdims=True))
        return ms

    def value_pass(qi, ms):
        r0 = qi * blk
        outs = []
        for h in range(2):
            acc = jnp.zeros((blk, LANES), F32)
            for lo, hi in chunks_of(qi):
                p = jnp.exp2(s_ref[qi % 2, h, :, lo:hi] - ms[h])
                acc = acc + _dot(p.astype(BF16), vh_ref[h, lo:hi, :])
            other = jnp.logical_not(first) if h == 0 else first
            l = jnp.sum(jnp.where(other, acc, 0.0), axis=-1, keepdims=True) * (1.0 / B_HEAD_DIM)
            outs.append(acc / l)
        o_ref[0, r0:r0 + blk, :] = jnp.where(first, outs[0], outs[1]).astype(o_ref.dtype)

    nq = seq // blk
    pending = score_pass(0)
    for qi in range(1, nq):
        ahead = score_pass(qi)
        value_pass(qi - 1, pending)
        pending = ahead
    value_pass(nq - 1, pending)


def ev_mix(proj, conv_w, q_g, k_g, *, batch, seq, blk=256, kchunk=256):
    nslab = A_WIDTH // LANES
    bias = _dilation_bias(seq, blk)
    qg2 = jnp.tile(q_g.reshape(1, -1), (1, LANES // B_HEAD_DIM))
    kg2 = jnp.tile(k_g.reshape(1, -1), (1, LANES // B_HEAD_DIM))

    def slab(off):
        return pl.BlockSpec((1, seq, LANES), lambda b, s: (b, 0, off * nslab + s))

    out_spec = pl.BlockSpec((1, seq, LANES), lambda b, s: (b, 0, s))
    return pl.pallas_call(
        functools.partial(_ev_mix_kernel, blk=blk, kchunk=kchunk),
        grid=(batch, nslab),
        in_specs=[slab(0), slab(1), slab(2), slab(3), slab(4), slab(5),
                  pl.BlockSpec((conv_w.shape[0], LANES), lambda b, s: (0, s)),
                  _resident((1, LANES)), _resident((1, LANES)), _resident(bias.shape)],
        out_specs=[out_spec, out_spec],
        out_shape=[jax.ShapeDtypeStruct((batch, seq, A_WIDTH), BF16)] * 2,
        scratch_shapes=[pltpu.VMEM((seq, LANES), BF16), pltpu.VMEM((2, seq, LANES), BF16),
                        pltpu.VMEM((2, seq, LANES), BF16), pltpu.VMEM((2, 2, blk, seq), F32)],
        compiler_params=_params("parallel", "parallel"),
        name="ev_mix",
    )(proj, proj, proj, proj, proj, proj, conv_w, qg2, kg2, bias)


def _mem_kv_kernel(m_ref, g_ref, w_ref, kg_ref, o_ref):
    width = X_HEADS * X_HEAD_DIM
    mn = _rms_rows(m_ref[0], g_ref[...]).astype(BF16)
    kv = _dot(mn, w_ref[...])
    k = kv[:, :width]
    o_ref[0, :, :width] = (k * _head_inv_rms(k, X_HEAD_DIM) * kg_ref[...]).astype(o_ref.dtype)
    o_ref[0, :, width:] = kv[:, width:].astype(o_ref.dtype)


def mem_kv(mem, g, w_kv, k_g):
    b, m, d = mem.shape
    width = w_kv.shape[1]
    kg = jnp.tile(k_g.reshape(1, -1), (1, X_HEADS))
    return pl.pallas_call(
        _mem_kv_kernel,
        grid=(b,),
        in_specs=[pl.BlockSpec((1, m, d), lambda i: (i, 0, 0)), _resident((1, d)), _resident(w_kv.shape),
                  _resident(kg.shape)],
        out_specs=pl.BlockSpec((1, m, width), lambda i: (i, 0, 0)),
        out_shape=jax.ShapeDtypeStruct((b, m, width), BF16),
        compiler_params=_params("parallel"),
        name="mem_kv",
    )(mem, g, w_kv, kg)


def _xattn_update(x, g_ref, wq_ref, qg_ref, kv_ref, wo_ref):
    width = X_HEADS * X_HEAD_DIM
    u = _rms_rows(x, g_ref[...]).astype(BF16)
    q = _dot(u, wq_ref[...])
    q = q * _head_inv_rms(q, X_HEAD_DIM) * (qg_ref[...] * X_HEAD_DIM ** -0.5)
    k = kv_ref[0, :, :width]
    v = kv_ref[0, :, width:]
    lane = lax.broadcasted_iota(jnp.int32, (1, width), 1)
    out = jnp.zeros(q.shape, F32)
    for h in range(X_HEADS):
        sel = (lane >= h * X_HEAD_DIM) & (lane < (h + 1) * X_HEAD_DIM)
        s = _dot_nt(jnp.where(sel, q, 0.0).astype(BF16), k)
        p = jnp.exp(s - jnp.max(s, axis=-1, keepdims=True))
        oh = _dot(p.astype(BF16), v) / jnp.sum(p, axis=-1, keepdims=True)
        out = jnp.where(sel, oh, out)
    return x + _dot(out.astype(BF16), wo_ref[...])


def _swiglu_act(u, wgu_ref, act_ref, d_ff, chunk):
    for lo in range(0, d_ff, chunk):
        hi = min(lo + chunk, d_ff)
        gate = _dot(u, wgu_ref[:, lo:hi])
        up = _dot(u, wgu_ref[:, d_ff + lo:d_ff + hi])
        act_ref[:, lo:hi] = (_silu(gate) * up).astype(act_ref.dtype)


def _tail_kernel(*refs, n_in, with_ffn, chunk):
    h_ref = refs[0]
    xs = refs[1:1 + n_in]
    ws = refs[1 + n_in:1 + 2 * n_in]
    xattn_refs = refs[1 + 2 * n_in:6 + 2 * n_in]
    x = h_ref[...]
    for x_ref, w_ref in zip(xs, ws):
        x = x + _dot(x_ref[...], w_ref[...])
    x = _xattn_update(x, *xattn_refs)
    if with_ffn:
        gf_ref, wgu_ref, wd_ref, o_ref, act_ref = refs[6 + 2 * n_in:]
        u = _rms_rows(x, gf_ref[...]).astype(BF16)
        _swiglu_act(u, wgu_ref, act_ref, wd_ref.shape[0], chunk)
        x = x + _dot(act_ref[...], wd_ref[...])
    else:
        o_ref = refs[-1]
    o_ref[...] = x


def layer_tail(h, xs, ws, g_x, w_q, q_g, kv, w_o, ffn_weights=None, *, seq, tm=512, chunk=256):
    n, d = h.shape
    qg = jnp.tile(q_g.reshape(1, -1), (1, X_HEADS))
    per_seq = seq // tm
    row = lambda i: (i, 0)
    in_specs = [pl.BlockSpec((tm, d), row)]
    in_specs += [pl.BlockSpec((tm, x.shape[1]), row) for x in xs]
    in_specs += [_resident(w.shape) for w in ws]
    in_specs += [_resident((1, d)), _resident(w_q.shape), _resident(qg.shape),
                 pl.BlockSpec((1,) + kv.shape[1:], lambda i: (i // per_seq, 0, 0)), _resident(w_o.shape)]
    args = [h, *xs, *ws, g_x, w_q, qg, kv, w_o]
    scratch = []
    if ffn_weights is not None:
        g_f, w_gu, w_down = ffn_weights
        in_specs += [_resident((1, d)), _resident(w_gu.shape), _resident(w_down.shape)]
        args += [g_f, w_gu, w_down]
        scratch = [pltpu.VMEM((tm, w_down.shape[0]), BF16)]
    return pl.pallas_call(
        functools.partial(_tail_kernel, n_in=len(xs), with_ffn=ffn_weights is not None, chunk=chunk),
        grid=(n // tm,),
        in_specs=in_specs,
        out_specs=pl.BlockSpec((tm, d), row),
        out_shape=jax.ShapeDtypeStruct((n, d), F32),
        scratch_shapes=scratch,
        compiler_params=_params("parallel"),
        name="layer_tail",
    )(*args)


def _split_bf16(x):
    hi = x.astype(BF16)
    return hi, (x - hi.astype(F32)).astype(BF16)


def _dot_split(x, w2_ref):
    half = w2_ref.shape[1] // 2
    xh, xl = _split_bf16(x)
    both = _dot(xh, w2_ref[...])
    return both[:, :half] + both[:, half:] + _dot(xl, w2_ref[:, :half])


def _split_weight(w):
    w = jnp.pad(w, ((0, 0), (0, LANES - w.shape[1])))
    hi = w.astype(BF16)
    return jnp.concatenate([hi, (w - hi.astype(F32)).astype(BF16)], axis=1)


def _moe_kernel(h_ref, g_ref, r2_ref, tri_ref, wgu_ref, wd_ref, o_ref,
                u_ref, rank_t_ref, wts_t_ref, act_ref, cnt_ref, *, chunk, sizes, sub_t):
    e = pl.program_id(1)
    n_sub = h_ref.shape[0] // sub_t
    d_ff = wd_ref.shape[1]
    lane = lax.broadcasted_iota(jnp.int32, (1, LANES), 1)

    @pl.when(e == 0)
    def _():
        x = h_ref[...]
        uf = _rms_rows(x, g_ref[...])
        u_ref[...] = uf.astype(BF16)
        logits = _dot_split(uf, r2_ref)
        logits = jnp.where(lane < N_EXPERTS, logits, NEG)
        m1 = jnp.max(logits, axis=-1, keepdims=True)
        i1 = jnp.min(jnp.where(logits == m1, lane, LANES), axis=-1, keepdims=True)
        rest = jnp.where(lane == i1, NEG, logits)
        m2 = jnp.max(rest, axis=-1, keepdims=True)
        i2 = jnp.min(jnp.where(rest == m2, lane, LANES), axis=-1, keepdims=True)
        e2 = jnp.exp(m2 - m1)
        w1 = 1.0 / (1.0 + e2)
        wts = jnp.where(lane == i1, w1, 0.0) + jnp.where(lane == i2, e2 * w1, 0.0)
        sel = jnp.where(lane == i1, 1.0, jnp.where(lane == i2, 1.0, 0.0))
        for s in range(n_sub):
            rs = slice(s * sub_t, (s + 1) * sub_t)
            sel_s = sel[rs]
            rank = jnp.where(sel_s > 0.0, _dot(tri_ref[...], sel_s.astype(BF16)), -1.0)
            rank_t_ref[s] = jnp.transpose(rank)
            wts_t_ref[s] = jnp.transpose(wts[rs])
            counts = jnp.sum(sel_s, axis=0, keepdims=True)
            for ee in range(N_EXPERTS):
                cnt_ref[s * N_EXPERTS + ee] = jnp.sum(jnp.where(lane == ee, counts, 0.0)).astype(jnp.int32)
        o_ref[...] = x

    def run_block(first, size, rs, rank_row, w_row):
        sub = lax.broadcasted_iota(jnp.int32, (size, 1), 0).astype(F32)
        act = act_ref.at[pl.ds(0, size)]
        hit = rank_row == first.astype(F32) + sub
        pick = jnp.where(hit, 1.0, 0.0).astype(BF16)
        w_rows = jnp.sum(jnp.where(hit, w_row, 0.0), axis=-1, keepdims=True)
        xg = _dot(pick, u_ref[rs, :]).astype(BF16)
        _swiglu_act(xg, wgu_ref.at[0], act, d_ff, chunk)
        y = (_dot(act[...], wd_ref[0]) * w_rows).astype(BF16)
        o_ref[rs, :] += _dot_tn(pick, y)

    big = sizes[-1]
    for s in range(n_sub):
        rs = slice(s * sub_t, (s + 1) * sub_t)
        args = (rs, rank_t_ref[s, pl.ds(e, 1), :], wts_t_ref[s, pl.ds(e, 1), :])
        count = cnt_ref[s * N_EXPERTS + e]
        n_full = count // big
        rem = count - n_full * big

        def full_block(b, carry, args=args):
            run_block(b * big, big, *args)
            return carry

        lax.fori_loop(0, n_full + (rem > sizes[-2]).astype(jnp.int32), full_block, 0)
        below = 0
        for size in sizes[:-1]:
            @pl.when(jnp.logical_and(rem > below, rem <= size))
            def _(size=size, args=args, n_full=n_full):
                run_block(n_full * big, size, *args)
            below = size


def moe(h, g, router, w_gu, w_down, *, tm=1024, sub_t=512, chunk=512, sizes=(128, 160, 192)):
    n, d = h.shape
    n_exp, d_ff, _ = w_down.shape
    n_sub = tm // sub_t
    tri = jnp.tril(jnp.ones((sub_t, sub_t), BF16), -1)
    r2 = _split_weight(router)
    return pl.pallas_call(
        functools.partial(_moe_kernel, chunk=chunk, sizes=sizes, sub_t=sub_t),
        grid=(n // tm, n_exp),
        in_specs=[pl.BlockSpec((tm, d), lambda i, e: (i, 0)), _resident((1, d)), _resident(r2.shape),
                  _resident(tri.shape),
                  pl.BlockSpec((1, d, 2 * d_ff), lambda i, e: (e, 0, 0)),
                  pl.BlockSpec((1, d_ff, d), lambda i, e: (e, 0, 0))],
        out_specs=pl.BlockSpec((tm, d), lambda i, e: (i, 0)),
        out_shape=jax.ShapeDtypeStruct((n, d), F32),
        scratch_shapes=[pltpu.VMEM((tm, d), BF16), pltpu.VMEM((n_sub, LANES, sub_t), F32),
                        pltpu.VMEM((n_sub, LANES, sub_t), F32), pltpu.VMEM((sizes[-1], d_ff), BF16),
                        pltpu.SMEM((n_sub * N_EXPERTS,), jnp.int32)],
        compiler_params=_params("parallel", "arbitrary"),
        name="moe",
    )(h, g, r2, tri, w_gu, w_down)


def _pair_mask(ii, jj, s):
    bi = ii // s
    return (bi % 2 == 1) & (jj // s == bi - 1)


def _bmm(a, b):
    return lax.dot_general(a, b, (((2,), (1,)), ((0,), (0,))), preferred_element_type=F32)


def _bmm_nt(a, b):
    return lax.dot_general(a, b, (((2,), (2,)), ((0,), (0,))), preferred_element_type=F32)


def _delta_kernel(q_ref, k_ref, v_ref, gate_ref, ba_ref, cq_ref, ck_ref, cv_ref, og_ref,
                  o_ref, p_s, n_s, egl_s, sall_s, xpad_s, *, heads):
    seq = q_ref.shape[1]
    ck = C_CHUNK
    nc = seq // ck
    dh = C_HEAD_DIM
    group = pl.program_id(1)
    lane = lax.broadcasted_iota(jnp.int32, (1, LANES), 1)
    ii = lax.broadcasted_iota(jnp.int32, (1, ck, ck), 1)
    jj = lax.broadcasted_iota(jnp.int32, (1, ck, ck), 2)
    eye = (ii == jj).astype(F32)

    gates = ba_ref[0]

    pad = xpad_s.shape[0] - seq
    xpad_s[0:pad, :] = jnp.zeros((pad, dh), F32)

    def conv_silu(x, w_ref, ls):
        kw = w_ref.shape[0]
        xpad_s[pad:, :] = x
        y = x * w_ref[kw - 1:kw, ls]
        for sh in range(1, kw):
            y = y + xpad_s[pad - sh:pad - sh + seq, :] * w_ref[kw - 1 - sh:kw - sh, ls]
        return _silu(y)

    def l2n(x):
        return x * lax.rsqrt(jnp.sum(x * x, axis=-1, keepdims=True) + EPS)

    prepared = []
    for g in range(heads):
        ls = slice(g * dh, (g + 1) * dh)
        head = group * heads + g
        q = l2n(conv_silu(q_ref[0, :, ls].astype(F32), cq_ref, ls)) * (dh ** -0.5)
        k = l2n(conv_silu(k_ref[0, :, ls].astype(F32), ck_ref, ls))
        v = conv_silu(v_ref[0, :, ls].astype(F32), cv_ref, ls)
        beta = jnp.broadcast_to(jnp.sum(jnp.where(lane == head, gates, 0.0), axis=-1, keepdims=True),
                                (seq, LANES))
        gcs = jnp.broadcast_to(jnp.sum(jnp.where(lane == head + C_HEADS, gates, 0.0), axis=-1, keepdims=True),
                               (seq, LANES))
        q3, k3, v3 = (t.reshape(nc, ck, dh) for t in (q, k, v))
        beta3 = beta.reshape(nc, ck, LANES)
        gcs3 = gcs.reshape(nc, ck, LANES)
        egc3 = jnp.exp(gcs3)
        glast = gcs3[:, ck - 1:ck, :]
        gc_row = jnp.swapaxes(gcs3, 1, 2)[:, :ck, :]
        decay = jnp.exp(jnp.where(ii >= jj, gcs3[:, :, :ck] - gc_row, NEG))
        kb = k3.astype(BF16)
        qk = _bmm_nt(jnp.concatenate([q3.astype(BF16), kb], axis=1), kb)
        attn = (qk[:, :ck] * decay).astype(BF16)
        a_mat = jnp.where(ii > jj, beta3[:, :, :ck] * qk[:, ck:] * decay, 0.0)
        t = eye - jnp.where(_pair_mask(ii, jj, 1), a_mat, 0.0)
        s = 2
        while s < ck:
            tb = t.astype(BF16)
            x = _bmm(tb, jnp.where(_pair_mask(ii, jj, s), a_mat, 0.0).astype(BF16))
            t = t - _bmm(x.astype(BF16), tb)
            s *= 2
        rhs = jnp.concatenate([v3 * beta3, k3 * (beta3 * egc3)], axis=2).astype(BF16)
        uw = _bmm(t.astype(BF16), rhs)
        kd_t = jnp.swapaxes(k3 * jnp.exp(glast - gcs3), 1, 2).astype(BF16)
        pn = _bmm(kd_t, uw.astype(BF16))
        n_s[g] = pn[:, :, :dh]
        p_s[g] = pn[:, :, dh:].astype(BF16)
        egl_s[g] = jnp.exp(glast)
        wq = jnp.concatenate([uw[:, :, dh:], q3 * egc3], axis=1).astype(BF16)
        prepared.append((wq, attn, uw[:, :, :dh]))

    def scan(c, states):
        new = []
        for g in range(heads):
            sb = states[g].astype(BF16)
            sall_s[g, c] = sb
            new.append(states[g] * egl_s[g, c] + n_s[g, c] - _dot(p_s[g, c], sb))
        return tuple(new)

    lax.fori_loop(0, nc, scan, tuple(jnp.zeros((dh, dh), F32) for _ in range(heads)))

    for g in range(heads):
        ls = slice(g * dh, (g + 1) * dh)
        wq, attn, u = prepared[g]
        r = _bmm(wq, sall_s[g])
        v_new = u - r[:, :ck]
        o = (r[:, ck:] + _bmm(attn, v_new.astype(BF16))).reshape(seq, dh)
        gate = gate_ref[0, :, ls].astype(F32)
        o_ref[0, :, ls] = (_rms_rows(o, og_ref[...]) * _silu(gate)).astype(o_ref.dtype)


def delta_mix(proj, ba, conv_w, o_g, *, batch, seq, heads=2):
    nh = C_HEADS
    ngroup = nh // heads
    width = heads * C_HEAD_DIM
    nc = seq // C_CHUNK

    def slab(off):
        return pl.BlockSpec((1, seq, width), lambda b, h: (b, 0, off * ngroup + h))

    def conv_slab(off):
        return pl.BlockSpec((conv_w.shape[0], width), lambda b, h: (0, off * ngroup + h))

    return pl.pallas_call(
        functools.partial(_delta_kernel, heads=heads),
        grid=(batch, ngroup),
        in_specs=[slab(0), slab(1), slab(2), slab(3),
                  pl.BlockSpec((1, seq, LANES), lambda b, h: (b, 0, 0)),
                  conv_slab(0), conv_slab(1), conv_slab(2), _resident((1, LANES))],
        out_specs=pl.BlockSpec((1, seq, width), lambda b, h: (b, 0, h)),
        out_shape=jax.ShapeDtypeStruct((batch, seq, nh * C_HEAD_DIM), BF16),
        scratch_shapes=[pltpu.VMEM((heads, nc, C_HEAD_DIM, C_HEAD_DIM), BF16),
                        pltpu.VMEM((heads, nc, C_HEAD_DIM, C_HEAD_DIM), F32),
                        pltpu.VMEM((heads, nc, 1, LANES), F32),
                        pltpu.VMEM((heads, nc, C_HEAD_DIM, C_HEAD_DIM), BF16),
                        pltpu.VMEM((seq + SUBLANES, C_HEAD_DIM), F32)],
        compiler_params=_params("parallel", "parallel"),
        name="delta_mix",
    )(proj, proj, proj, proj, ba, conv_w, conv_w, conv_w, o_g.reshape(1, LANES))


def kernel(x, mem, norm_mix, norm_xattn, norm_mem, norm_ffn, ev_w_in, ev_conv, ev_q_norm, ev_k_norm, ev_w_out, od_w_in, od_conv, od_a_log, od_dt_bias, od_o_norm, od_w_out, xa_w_q, xa_w_kv, xa_q_norm, xa_k_norm, xa_w_o, ff_w_gu, ff_w_down, moe_router, moe_w_gu, moe_w_down):
    batch, seq, d = x.shape
    n = batch * seq
    h = x.reshape(n, d)
    depth = norm_mix.shape[0]
    for layer in range(depth):
        i = layer // 2
        g_mix = norm_mix[layer].reshape(1, d)
        g_ffn = norm_ffn[layer].reshape(1, d)
        kv = mem_kv(mem, norm_mem[layer].reshape(1, d), xa_w_kv[layer].astype(BF16), xa_k_norm[layer])
        xattn_args = (norm_xattn[layer].reshape(1, d), xa_w_q[layer].astype(BF16), xa_q_norm[layer], kv,
                      xa_w_o[layer].astype(BF16))
        if layer % 2 == 0:
            proj = norm_proj(h, g_mix, ev_w_in[i].astype(BF16))
            a, o = ev_mix(proj.reshape(batch, seq, -1), ev_conv[i], ev_q_norm[i], ev_k_norm[i],
                          batch=batch, seq=seq)
            w_out = ev_w_out[i].astype(BF16)
            h = layer_tail(h, [a.reshape(n, -1), o.reshape(n, -1)], [w_out[:A_WIDTH], w_out[A_WIDTH:]],
                           *xattn_args, (g_ffn, ff_w_gu[i].astype(BF16), ff_w_down[i].astype(BF16)), seq=seq)
        else:
            n_main = 4 * C_HEADS * C_HEAD_DIM
            w_gates = _split_weight(od_w_in[i][:, n_main:])
            proj, ba = norm_proj(h, g_mix, od_w_in[i][:, :n_main].astype(BF16), w_gates, od_a_log[i],
                                 od_dt_bias[i])
            o = delta_mix(proj.reshape(batch, seq, -1), ba.reshape(batch, seq, -1), od_conv[i], od_o_norm[i],
                          batch=batch, seq=seq)
            h = layer_tail(h, [o.reshape(n, -1)], [od_w_out[i].astype(BF16)], *xattn_args, seq=seq, tm=1024)
            h = moe(h, g_ffn, moe_router[i], moe_w_gu[i].astype(BF16), moe_w_down[i].astype(BF16))
    return h.reshape(batch, seq, d)
```
